```python
import jax, jax.numpy as jnp
from jax import lax
import numpy as np

D_MODEL = 1024
BATCH = 16
SEQ = 2048
DEPTH = 2

HEAD_DIM = 64
GLA_HEADS = 4
GLA_DK = 64
GLA_DV = 128
GLA_GATE_RANK = 16
GLA_GATE_NORM = 16.0
GLA_CHUNK = 64
MOBA_HEADS = 4
MOBA_BLOCK = 256
MOBA_TOPK = 3
MOBA_Q_CHUNK = 16
DSA_HEADS = 4
DSA_TOPK = 256
IDX_HEADS = 8
IDX_DIM = 64
DSA_Q_CHUNK = 64
ROPE_THETA = 500000.0
ROPE_FRACTION = 4
MIX_WIDTH = GLA_HEADS * GLA_DV + MOBA_HEADS * HEAD_DIM + DSA_HEADS * HEAD_DIM
N_EXPERTS = 16
N_GROUPS = 4
EXPERTS_PER_GROUP = N_EXPERTS // N_GROUPS
TOP_K = 2
D_FF = 512
DN_ALPHA = (2 * DEPTH) ** 0.25
DN_BETA = (8 * DEPTH) ** -0.25
LN_EPS = 1e-5
RMS_EPS = 1e-6

IN_WIDTHS = (
    GLA_HEADS * GLA_DK, GLA_HEADS * GLA_DK, GLA_HEADS * GLA_DV, GLA_GATE_RANK, GLA_HEADS * GLA_DV,
    MOBA_HEADS * HEAD_DIM, MOBA_HEADS * HEAD_DIM, MOBA_HEADS * HEAD_DIM,
    DSA_HEADS * HEAD_DIM, DSA_HEADS * HEAD_DIM, DSA_HEADS * HEAD_DIM,
    IDX_HEADS * IDX_DIM, IDX_DIM, IDX_HEADS,
)
IN_WIDTH = sum(IN_WIDTHS)

kernel_name = 'hybrid_gla_moba_dsa_moe'


def layer_norm(x, g, b):
    xf = x.astype(jnp.float32)
    mu = jnp.mean(xf, -1, keepdims=True)
    var = jnp.mean(jnp.square(xf - mu), -1, keepdims=True)
    y = (xf - mu) * lax.rsqrt(var + LN_EPS) * g.astype(jnp.float32) + b.astype(jnp.float32)
    return y.astype(x.dtype)


def rms_norm(x, g):
    xf = x.astype(jnp.float32)
    y = xf * lax.rsqrt(jnp.mean(jnp.square(xf), -1, keepdims=True) + RMS_EPS) * g.astype(jnp.float32)
    return y.astype(x.dtype)


def split_columns(y, widths):
    parts, start = [], 0
    for w in widths:
        parts.append(y[..., start:start + w])
        start += w
    return parts


def partial_rope(x, positions):
    rot = x.shape[-1] // ROPE_FRACTION
    half = rot // 2
    inv_freq = ROPE_THETA ** (-jnp.arange(half, dtype=jnp.float32) / half)
    ang = positions.astype(jnp.float32)[:, :, None, None] * inv_freq
    cos, sin = jnp.cos(ang), jnp.sin(ang)
    xf = x.astype(jnp.float32)
    x1, x2, rest = xf[..., :half], xf[..., half:rot], xf[..., rot:]
    out = jnp.concatenate([x1 * cos - x2 * sin, x2 * cos + x1 * sin, rest], -1)
    return out.astype(x.dtype)


def gla_mixer(q, k, v, gate_rank, g_out, gate_up, gate_bias, norm_gain):
    B, L, _ = q.shape
    f32 = jnp.float32
    C = GLA_CHUNK
    N = L // C
    log_a = jax.nn.log_sigmoid((gate_rank @ gate_up + gate_bias).astype(f32)) / GLA_GATE_NORM

    def heads(t, d):
        return t.astype(f32).reshape(B, N, C, GLA_HEADS, d).transpose(0, 3, 1, 2, 4)

    qh = heads(q, GLA_DK) * GLA_DK ** -0.5
    kh = heads(k, GLA_DK)
    vh = heads(v, GLA_DV)
    b = jnp.cumsum(heads(log_a, GLA_DK), axis=3)
    b_last = b[:, :, :, -1:, :]
    q_dec = qh * jnp.exp(b)
    k_inv = kh * jnp.exp(-b)
    causal = jnp.tril(jnp.ones((C, C), bool))
    a = jnp.where(causal, jnp.einsum('bhnid,bhnjd->bhnij', q_dec, k_inv), 0.0)
    o_intra = jnp.einsum('bhnij,bhnjv->bhniv', a, vh)
    kv = jnp.einsum('bhncd,bhncv->bhndv', kh * jnp.exp(b_last - b), vh)
    decay = jnp.exp(b_last[:, :, :, 0, :])

    def step(S, inp):
        kv_n, dec_n = inp
        return dec_n[..., None] * S + kv_n, S

    S0 = jnp.zeros((B, GLA_HEADS, GLA_DK, GLA_DV), f32)
    _, S_in = lax.scan(step, S0, (jnp.moveaxis(kv, 2, 0), jnp.moveaxis(decay, 2, 0)))
    S_in = jnp.moveaxis(S_in, 0, 2)
    o = o_intra + jnp.einsum('bhncd,bhndv->bhncv', q_dec, S_in)
    o = o.transpose(0, 2, 3, 1, 4).reshape(B, L, GLA_HEADS, GLA_DV)
    o = rms_norm(o, norm_gain).reshape(B, L, GLA_HEADS * GLA_DV)
    return (o * jax.nn.silu(g_out.astype(f32))).astype(q.dtype)


def moba_mixer(q, k, v):
    B, L, H, d = q.shape
    f32 = jnp.float32
    NB = -(-L // MOBA_BLOCK)
    Lp = NB * MOBA_BLOCK
    n_sel = min(MOBA_TOPK, NB - 1)
    qh = (q * d ** -0.5).transpose(0, 2, 1, 3)
    pad = ((0, 0), (0, 0), (0, Lp - L), (0, 0))
    kb = jnp.pad(k.transpose(0, 2, 1, 3), pad).reshape(B, H, NB, MOBA_BLOCK, d)
    vb = jnp.pad(v.transpose(0, 2, 1, 3), pad).reshape(B, H, NB, MOBA_BLOCK, d)
    k_mean = jnp.mean(kb.astype(f32), axis=3).astype(q.dtype)
    n_chunks = L // MOBA_Q_CHUNK
    q_chunks = jnp.moveaxis(qh.reshape(B, H, n_chunks, MOBA_Q_CHUNK, d), 2, 0)
    gather = jax.vmap(jax.vmap(lambda blocks, idx: blocks[idx]))

    def one_chunk(args):
        qc, c = args
        pos_q = c * MOBA_Q_CHUNK + jnp.arange(MOBA_Q_CHUNK)
        blk = (c * MOBA_Q_CHUNK) // MOBA_BLOCK
        k_own = lax.dynamic_index_in_dim(kb, blk, axis=2, keepdims=False)
        v_own = lax.dynamic_index_in_dim(vb, blk, axis=2, keepdims=False)
        pos_k = blk * MOBA_BLOCK + jnp.arange(MOBA_BLOCK)
        s_own = jnp.einsum('bhqd,bhkd->bhqk', qc, k_own).astype(f32)
        s_own = jnp.where(pos_k[None, :] <= pos_q[:, None], s_own, -jnp.inf)
        if n_sel == 0:
            p = jax.nn.softmax(s_own, -1).astype(v.dtype)
            return jnp.einsum('bhqk,bhkd->bhqd', p, v_own)
        gate = jnp.einsum('bhqd,bhnd->bhqn', qc, k_mean).astype(f32)
        gate = jnp.where(jnp.arange(NB) < blk, gate, -jnp.inf)
        _, idx = lax.top_k(gate, n_sel)
        valid = jnp.arange(n_sel) < blk
        k_sel = gather(kb, idx)
        v_sel = gather(vb, idx)
        s_sel = jnp.einsum('bhqd,bhqnkd->bhqnk', qc, k_sel).astype(f32)
        s_sel = jnp.where(valid[:, None], s_sel, -jnp.inf).reshape(B, H, MOBA_Q_CHUNK, n_sel * MOBA_BLOCK)
        p = jax.nn.softmax(jnp.concatenate([s_sel, s_own], -1), -1).astype(v.dtype)
        p_sel = p[..., :n_sel * MOBA_BLOCK].reshape(B, H, MOBA_Q_CHUNK, n_sel, MOBA_BLOCK)
        p_own = p[..., n_sel * MOBA_BLOCK:]
        return (jnp.einsum('bhqnk,bhqnkd->bhqd', p_sel, v_sel)
                + jnp.einsum('bhqk,bhkd->bhqd', p_own, v_own))

    out = lax.map(one_chunk, (q_chunks, jnp.arange(n_chunks)))
    out = jnp.moveaxis(out, 0, 2).reshape(B, H, L, d).transpose(0, 2, 1, 3)
    return out.reshape(B, L, H * d)


def dsa_mixer(q, k, v, q_idx, k_idx, w_idx):
    B, L, H, d = q.shape
    f32 = jnp.float32
    top = min(DSA_TOPK, L // 4)
    n_chunks = L // DSA_Q_CHUNK

    def chunks(t):
        return jnp.moveaxis(t.reshape((B, n_chunks, DSA_Q_CHUNK) + t.shape[2:]), 1, 0)

    qs = chunks(q * d ** -0.5)
    qis = chunks(q_idx)
    ws = chunks(w_idx)
    kh = k.transpose(0, 2, 1, 3)
    vh = v.transpose(0, 2, 1, 3)
    k_idx_f = k_idx.astype(f32)
    gather = jax.vmap(lambda t, idx: t[:, idx])
    pos_k = jnp.arange(L)

    def one_chunk(args):
        qc, qic, wc, c = args
        pos_q = c * DSA_Q_CHUNK + jnp.arange(DSA_Q_CHUNK)
        logits = jnp.einsum('bqhe,bse->bqhs', qic.astype(f32), k_idx_f) * IDX_DIM ** -0.5
        iscore = jnp.einsum('bqhs,bqh->bqs', jax.nn.relu(logits), wc.astype(f32) * IDX_HEADS ** -0.5)
        iscore = jnp.where(pos_k[None, :] <= pos_q[:, None], iscore, -jnp.inf)
        _, idx = lax.top_k(iscore, top)
        valid = idx <= pos_q[None, :, None]
        k_sel = gather(kh, idx)
        v_sel = gather(vh, idx)
        s = jnp.einsum('bqhd,bhqkd->bhqk', qc, k_sel).astype(f32)
        s = jnp.where(valid[:, None], s, -jnp.inf)
        p = jax.nn.softmax(s, -1).astype(v.dtype)
        return jnp.einsum('bhqk,bhqkd->bqhd', p, v_sel)

    out = lax.map(one_chunk, (qs, qis, ws, jnp.arange(n_chunks)))
    return jnp.moveaxis(out, 0, 1).reshape(B, L, H * d)


def token_mixers(h, positions, w_in, gla_gate_up, gla_gate_bias, gla_norm_gain, w_out):
    B, L, _ = h.shape
    y = h @ w_in
    (g_q, g_k, g_v, g_rank, g_gate, m_q, m_k, m_v,
     s_q, s_k, s_v, i_q, i_k, i_w) = split_columns(y, IN_WIDTHS)
    out_a = gla_mixer(g_q, g_k, g_v, g_rank, g_gate, gla_gate_up, gla_gate_bias, gla_norm_gain)

    def hd(t, n, d):
        return t.reshape(B, L, n, d)

    out_b = moba_mixer(partial_rope(hd(m_q, MOBA_HEADS, HEAD_DIM), positions),
                       partial_rope(hd(m_k, MOBA_HEADS, HEAD_DIM), positions),
                       hd(m_v, MOBA_HEADS, HEAD_DIM))
    out_c = dsa_mixer(partial_rope(hd(s_q, DSA_HEADS, HEAD_DIM), positions),
                      partial_rope(hd(s_k, DSA_HEADS, HEAD_DIM), positions),
                      hd(s_v, DSA_HEADS, HEAD_DIM),
                      partial_rope(hd(i_q, IDX_HEADS, IDX_DIM), positions),
                      partial_rope(hd(i_k, 1, IDX_DIM), positions)[:, :, 0],
                      i_w)
    mixed = jnp.concatenate([out_a, out_b, out_c], -1)
    return mixed @ w_out


def moe_ffn(h, router_w, router_bias, w_gate, w_up, w_down):
    B, L, D = h.shape
    f32 = jnp.float32
    t = h.reshape(B * L, D)
    scores = jax.nn.sigmoid((t @ router_w).astype(f32))
    sel = scores + router_bias.astype(f32)
    grouped = sel.reshape(-1, N_GROUPS, EXPERTS_PER_GROUP)
    group_score = jnp.sum(lax.top_k(grouped, TOP_K)[0], -1)
    best = jnp.argmax(group_score, -1)
    in_group = (jnp.arange(N_EXPERTS) // EXPERTS_PER_GROUP)[None, :] == best[:, None]
    _, top_idx = lax.top_k(jnp.where(in_group, sel, -jnp.inf), TOP_K)
    top_w = jnp.take_along_axis(scores, top_idx, -1)
    top_w = top_w / jnp.sum(top_w, -1, keepdims=True)
    combine = jnp.einsum('tk,tke->te', top_w, jax.nn.one_hot(top_idx, N_EXPERTS, dtype=f32)).astype(h.dtype)
    out = jnp.zeros_like(t)
    for e in range(N_EXPERTS):
        act = jax.nn.silu(t @ w_gate[e]) * (t @ w_up[e])
        out = out + combine[:, e:e + 1] * (act @ w_down[e])
    return out.reshape(B, L, D)


def setup_inputs(seed: int = 0) -> dict:
    key = jax.random.key(seed)
    ks = jax.random.split(key, 16)
    f32 = jnp.float32

    def normal(k, shape, scale):
        return jax.random.normal(k, shape, f32) * scale

    x = normal(ks[0], (BATCH, SEQ, D_MODEL), 1.0)
    offsets = jax.random.randint(ks[1], (BATCH, 1), 0, 4096, dtype=jnp.int32)
    positions = offsets + jnp.arange(SEQ, dtype=jnp.int32)[None, :]
    w_in = normal(ks[2], (DEPTH, D_MODEL, IN_WIDTH), D_MODEL ** -0.5)
    gla_gate_up = normal(ks[3], (DEPTH, GLA_GATE_RANK, GLA_HEADS * GLA_DK), GLA_GATE_RANK ** -0.5)
    gla_gate_bias = normal(ks[4], (DEPTH, GLA_HEADS * GLA_DK), 0.1)
    gla_norm_gain = 1.0 + normal(ks[5], (DEPTH, GLA_DV), 0.01)
    w_out = normal(ks[6], (DEPTH, MIX_WIDTH, D_MODEL), MIX_WIDTH ** -0.5 * DN_BETA)
    ln_mix_g = 1.0 + normal(ks[7], (DEPTH, D_MODEL), 0.01)
    ln_mix_b = normal(ks[8], (DEPTH, D_MODEL), 0.01)
    router_w = normal(ks[9], (D_MODEL, N_EXPERTS), D_MODEL ** -0.5)
    router_bias = normal(ks[10], (N_EXPERTS,), 0.01)
    w_expert_gate = normal(ks[11], (DEPTH, N_EXPERTS, D_MODEL, D_FF), D_MODEL ** -0.5)
    w_expert_up = normal(ks[12], (DEPTH, N_EXPERTS, D_MODEL, D_FF), D_MODEL ** -0.5)
    w_expert_down = normal(ks[13], (DEPTH, N_EXPERTS, D_FF, D_MODEL), D_FF ** -0.5 * DN_BETA)
    ln_ffn_g = 1.0 + normal(ks[14], (DEPTH, D_MODEL), 0.01)
    ln_ffn_b = normal(ks[15], (DEPTH, D_MODEL), 0.01)
    return {'x': x, 'positions': positions, 'w_in': w_in, 'gla_gate_up': gla_gate_up,
            'gla_gate_bias': gla_gate_bias, 'gla_norm_gain': gla_norm_gain, 'w_out': w_out,
            'ln_mix_g': ln_mix_g, 'ln_mix_b': ln_mix_b, 'router_w': router_w,
            'router_bias': router_bias, 'w_expert_gate': w_expert_gate,
            'w_expert_up': w_expert_up, 'w_expert_down': w_expert_down,
            'ln_ffn_g': ln_ffn_g, 'ln_ffn_b': ln_ffn_b}


def reference(x, positions, w_in, gla_gate_up, gla_gate_bias, gla_norm_gain, w_out,
              ln_mix_g, ln_mix_b, router_w, router_bias, w_expert_gate, w_expert_up,
              w_expert_down, ln_ffn_g, ln_ffn_b):
    h = x
    for layer in range(DEPTH):
        mix = token_mixers(h, positions, w_in[layer], gla_gate_up[layer], gla_gate_bias[layer],
                           gla_norm_gain[layer], w_out[layer])
        h = layer_norm(DN_ALPHA * h + mix, ln_mix_g[layer], ln_mix_b[layer])
        ffn = moe_ffn(h, router_w, router_bias, w_expert_gate[layer], w_expert_up[layer],
                      w_expert_down[layer])
        h = layer_norm(DN_ALPHA * h + ffn, ln_ffn_g[layer], ln_ffn_b[layer])
    return h
```

```python
import functools

import jax
import jax.numpy as jnp
from jax import lax
from jax.experimental import pallas as pl
from jax.experimental.pallas import tpu as pltpu

F32 = jnp.float32
BF16 = jnp.bfloat16
HIGHEST = lax.Precision.HIGHEST

HEAD_DIM = 64
GLA_HEADS, GLA_DK, GLA_DV = 4, 64, 128
GLA_GATE_RANK = 16
GLA_GATE_NORM = 16.0
GLA_CHUNK = 64
MOBA_HEADS, MOBA_BLOCK, MOBA_TOPK = 4, 256, 3
DSA_HEADS, DSA_TOPK = 4, 256
IDX_HEADS, IDX_DIM = 8, 64
ROPE_THETA = 500000.0
ROPE_HALF = 8
N_EXPERTS, N_GROUPS, TOP_K = 16, 4, 2
EXPERTS_PER_GROUP = N_EXPERTS // N_GROUPS
LN_EPS = 1e-5
RMS_EPS = 1e-6

LANES = 128
VMEM_LIMIT = 56 * 1024 * 1024

_SRC = dict(g_q=(0, 256), g_k=(256, 256), g_v=(512, 512), g_rank=(1024, 16), g_gate=(1040, 512),
            m_q=(1552, 256), m_k=(1808, 256), m_v=(2064, 256),
            s_q=(2320, 256), s_k=(2576, 256), s_v=(2832, 256),
            i_q=(3088, 512), i_k=(3600, 64), i_w=(3664, 8))
_DST = dict(g_q=(0, 256), g_k=(256, 256), g_v=(512, 512), g_gate=(1024, 512),
            m_q=(1536, 256), m_k=(1792, 256), m_v=(2048, 256),
            s_q=(2304, 256), s_k=(2560, 256), s_v=(2816, 256),
            i_q=(3072, 512), i_k=(3584, 128), misc=(3712, 128))
NP = 3840
MISC_RANK0 = 0
MISC_W0 = 16
_ROPED = ("m_q", "m_k", "s_q", "s_k", "i_q", "i_k")
_SCALED = dict(g_q=GLA_DK ** -0.5, m_q=HEAD_DIM ** -0.5, s_q=HEAD_DIM ** -0.5, i_q=IDX_DIM ** -0.5)


def _cparams(sem):
    return pltpu.CompilerParams(dimension_semantics=sem, vmem_limit_bytes=VMEM_LIMIT)


def _dot(a, b):
    return jnp.dot(a, b, preferred_element_type=F32)


def _dot_hi(a, b):
    return jnp.dot(a, b, precision=HIGHEST, preferred_element_type=F32)


def _dot_nt(a, b):
    return lax.dot_general(a, b, (((1,), (1,)), ((), ())), preferred_element_type=F32)


def _dot_tn(a, b):
    return lax.dot_general(a, b, (((0,), (0,)), ((), ())), preferred_element_type=F32)


def _layer_norm(x, g, b):
    mu = jnp.mean(x, -1, keepdims=True)
    xc = x - mu
    var = jnp.mean(xc * xc, -1, keepdims=True)
    return xc * lax.rsqrt(var + LN_EPS) * g + b


def _silu(x):
    return x * (1.0 / (1.0 + jnp.exp(-x)))


def _rope_table_kernel(pos_ref, inv_ref, c_ref, s1_ref, s2_ref):
    ang = pos_ref[...].astype(F32) * inv_ref[...]
    lane = lax.broadcasted_iota(jnp.int32, ang.shape, 1) % HEAD_DIM
    cos, sin = jnp.cos(ang), jnp.sin(ang)
    first = lane < ROPE_HALF
    second = (lane >= ROPE_HALF) & (lane < 2 * ROPE_HALF)
    c_ref[...] = jnp.where(first | second, cos, 1.0)
    s1_ref[...] = jnp.where(first, -sin, 0.0)
    s2_ref[...] = jnp.where(second, sin, 0.0)


def _rope_tables(positions, tm=512):
    T = positions.size
    pos = positions.reshape(T, 1)
    j = (jnp.arange(LANES) % ROPE_HALF).astype(F32)
    inv = (ROPE_THETA ** (-j / ROPE_HALF)).reshape(1, LANES)
    out = jax.ShapeDtypeStruct((T, LANES), F32)
    return pl.pallas_call(
        _rope_table_kernel,
        grid=(T // tm,),
        in_specs=[pl.BlockSpec((tm, 1), lambda i: (i, 0)), pl.BlockSpec((1, LANES), lambda i: (0, 0))],
        out_specs=[pl.BlockSpec((tm, LANES), lambda i: (i, 0))] * 3,
        out_shape=[out, out, out],
        compiler_params=_cparams(("parallel",)),
        name="rope_tables",
    )(pos, inv)


def _in_proj_kernel(h_ref, w_ref, c_ref, s1_ref, s2_ref, y_ref):
    hb = h_ref[...].astype(BF16)
    c, s1, s2 = c_ref[...], s1_ref[...], s2_ref[...]
    for name, (start, width) in _DST.items():
        y = _dot(hb, w_ref[:, start:start + width])
        if name in _SCALED:
            y = y * _SCALED[name]
        if name in _ROPED:
            blocks = []
            for o in range(0, width, LANES):
                yb = y[:, o:o + LANES]
                up = pltpu.roll(yb, LANES - ROPE_HALF, 1)
                down = pltpu.roll(yb, ROPE_HALF, 1)
                blocks.append(yb * c + up * s1 + down * s2)
            y = blocks[0] if len(blocks) == 1 else jnp.concatenate(blocks, axis=1)
        y_ref[:, start:start + width] = y


def _pack_w_in(w_in):
    D = w_in.shape[0]
    cols = []
    for name, (start, width) in _DST.items():
        if name == "misc":
            r0, rw = _SRC["g_rank"]
            w0, ww = _SRC["i_w"]
            blk = jnp.zeros((D, width), w_in.dtype)
            blk = blk.at[:, MISC_RANK0:MISC_RANK0 + rw].set(w_in[:, r0:r0 + rw])
            blk = blk.at[:, MISC_W0:MISC_W0 + ww].set(w_in[:, w0:w0 + ww])
        else:
            s0, sw = _SRC[name]
            blk = w_in[:, s0:s0 + sw]
            if sw < width:
                blk = jnp.pad(blk, ((0, 0), (0, width - sw)))
        cols.append(blk)
    return jnp.concatenate(cols, axis=1).astype(BF16)


def _in_proj(h, w_packed, tabs, tm=512):
    T, D = h.shape
    tab_spec = pl.BlockSpec((tm, LANES), lambda i: (i, 0))
    return pl.pallas_call(
        _in_proj_kernel,
        grid=(T // tm,),
        in_specs=[pl.BlockSpec((tm, D), lambda i: (i, 0)),
                  pl.BlockSpec((D, NP), lambda i: (0, 0)),
                  tab_spec, tab_spec, tab_spec],
        out_specs=pl.BlockSpec((tm, NP), lambda i: (i, 0)),
        out_shape=jax.ShapeDtypeStruct((T, NP), F32),
        compiler_params=_cparams(("parallel",)),
        name="in_proj",
    )(h, w_packed, *tabs)


def _gla_kernel(q_ref, k_ref, v_ref, g_ref, misc_ref, up_ref, bias_ref, gain_ref, o_ref, st_ref, *, tl):
    @pl.when(pl.program_id(1) == 0)
    def _():
        st_ref[...] = jnp.zeros_like(st_ref)

    C = GLA_CHUNK
    dkw, dvw = GLA_HEADS * GLA_DK, GLA_HEADS * GLA_DV
    z = _dot_hi(misc_ref[...], up_ref[...]) + bias_ref[...]
    log_a = (jnp.minimum(z, 0.0) - jnp.log(1.0 + jnp.exp(-jnp.abs(z)))) / GLA_GATE_NORM
    ri = lax.broadcasted_iota(jnp.int32, (C, C), 0)
    ci = lax.broadcasted_iota(jnp.int32, (C, C), 1)
    causal = ci <= ri
    tri = jnp.where(causal, 1.0, 0.0)
    lane_head = lax.broadcasted_iota(jnp.int32, (C, dkw), 1) // GLA_DK
    sr = lax.broadcasted_iota(jnp.int32, (dvw, dkw), 0) // GLA_DV
    sc = lax.broadcasted_iota(jnp.int32, (dvw, dkw), 1) // GLA_DK
    diag = sr == sc
    gain = gain_ref[...]

    for c in range(tl // C):
        rows = slice(c * C, (c + 1) * C)
        b = _dot_hi(tri, log_a[rows])
        b_last = b[C - 1:C, :]
        q_dec = q_ref[rows, :] * jnp.exp(b)
        k_c = k_ref[rows, :]
        k_inv = (k_c * jnp.exp(-b)).astype(BF16)
        k_dec = (k_c * jnp.exp(b_last - b)).astype(BF16)
        v_c = v_ref[rows, :].astype(BF16)
        q_dec_b = q_dec.astype(BF16)
        st = st_ref[...]
        o = _dot_nt(q_dec_b, st.astype(BF16))
        intra = []
        for h in range(GLA_HEADS):
            qh = jnp.where(lane_head == h, q_dec, 0.0).astype(BF16)
            a = jnp.where(causal, _dot_nt(qh, k_inv), 0.0)
            intra.append(_dot(a.astype(BF16), v_c[:, h * GLA_DV:(h + 1) * GLA_DV]))
        o = o + jnp.concatenate(intra, axis=1)
        kv_t = _dot_tn(v_c, k_dec)
        st_ref[...] = jnp.exp(b_last) * st + jnp.where(diag, kv_t, 0.0)
        outs = []
        for h in range(GLA_HEADS):
            oh = o[:, h * GLA_DV:(h + 1) * GLA_DV]
            ms = jnp.mean(oh * oh, -1, keepdims=True)
            outs.append(oh * lax.rsqrt(ms + RMS_EPS) * gain)
        o_ref[rows, :] = jnp.concatenate(outs, axis=1) * _silu(g_ref[rows, :])


def _gla(y, B, L, gate_up, gate_bias, norm_gain, tl=512):
    T = B * L
    nt = L // tl
    up = jnp.zeros((LANES, GLA_HEADS * GLA_DK), F32).at[MISC_RANK0:MISC_RANK0 + GLA_GATE_RANK].set(gate_up)
    row = lambda b, t: b * nt + t
    return pl.pallas_call(
        functools.partial(_gla_kernel, tl=tl),
        grid=(B, nt),
        in_specs=[pl.BlockSpec((tl, 256), lambda b, t: (row(b, t), _DST["g_q"][0] // 256)),
                  pl.BlockSpec((tl, 256), lambda b, t: (row(b, t), _DST["g_k"][0] // 256)),
                  pl.BlockSpec((tl, 512), lambda b, t: (row(b, t), _DST["g_v"][0] // 512)),
                  pl.BlockSpec((tl, 512), lambda b, t: (row(b, t), _DST["g_gate"][0] // 512)),
                  pl.BlockSpec((tl, LANES), lambda b, t: (row(b, t), _DST["misc"][0] // LANES)),
                  pl.BlockSpec((LANES, 256), lambda b, t: (0, 0)),
                  pl.BlockSpec((1, 256), lambda b, t: (0, 0)),
                  pl.BlockSpec((1, GLA_DV), lambda b, t: (0, 0))],
        out_specs=pl.BlockSpec((tl, 512), lambda b, t: (row(b, t), 0)),
        out_shape=jax.ShapeDtypeStruct((T, GLA_HEADS * GLA_DV), F32),
        scratch_shapes=[pltpu.VMEM((GLA_HEADS * GLA_DV, GLA_HEADS * GLA_DK), F32)],
        compiler_params=_cparams(("parallel", "arbitrary")),
        name="gla",
    )(y, y, y, y, y, up, gate_bias.reshape(1, -1), norm_gain.reshape(1, -1))


def _moba_kernel(q_ref, k_ref, v_ref, o_ref, *, qb):
    BLK = MOBA_BLOCK
    qt = q_ref[...].T
    feat_head = lax.broadcasted_iota(jnp.int32, qt.shape, 0) // HEAD_DIM
    krow = lax.broadcasted_iota(jnp.int32, (BLK, BLK), 0)
    qcol = lax.broadcasted_iota(jnp.int32, (BLK, BLK), 1)
    own_bias = jnp.where(krow <= qcol, 0.0, -jnp.inf)
    k_blocks = [k_ref[j * BLK:(j + 1) * BLK, :] for j in range(qb + 1)]
    kb_blocks = [kb.astype(BF16) for kb in k_blocks]
    vt_blocks = [v_ref[j * BLK:(j + 1) * BLK, :].T.astype(BF16) for j in range(qb + 1)]
    if qb > 0:
        k_mean = jnp.concatenate([jnp.mean(kb, axis=0, keepdims=True) for kb in k_blocks[:qb]], axis=0)
    outs = []
    for h in range(MOBA_HEADS):
        qth = jnp.where(feat_head == h, qt, 0.0)
        qth_b = qth.astype(BF16)
        biases = []
        if qb > 0:
            gate = _dot_hi(k_mean, qth)
            g = [gate[j:j + 1, :] for j in range(qb)]
            for j in range(qb):
                rank = jnp.zeros_like(g[j])
                for i in range(qb):
                    if i == j:
                        continue
                    ahead = (g[i] >= g[j]) if i < j else (g[i] > g[j])
                    rank = rank + jnp.where(ahead, 1.0, 0.0)
                biases.append(jnp.where(rank < MOBA_TOPK, 0.0, -jnp.inf))
        s = [_dot(kb_blocks[j], qth_b) + biases[j] for j in range(qb)]
        s.append(_dot(kb_blocks[qb], qth_b) + own_bias)
        m = jnp.max(s[0], axis=0, keepdims=True)
        for sj in s[1:]:
            m = jnp.maximum(m, jnp.max(sj, axis=0, keepdims=True))
        den = jnp.zeros_like(m)
        acc = jnp.zeros((HEAD_DIM, BLK), F32)
        for j in range(qb + 1):
            p = jnp.exp(s[j] - m)
            den = den + jnp.sum(p, axis=0, keepdims=True)
            acc = acc + _dot(vt_blocks[j][h * HEAD_DIM:(h + 1) * HEAD_DIM, :], p.astype(BF16))
        outs.append(acc / den)
    o_ref[...] = jnp.concatenate(outs, axis=0).T


def _moba(y3):
    B, L, _ = y3.shape
    BLK = MOBA_BLOCK
    nb = L // BLK
    outs = []
    for qb in range(nb):
        S = (qb + 1) * BLK
        outs.append(pl.pallas_call(
            functools.partial(_moba_kernel, qb=qb),
            grid=(B,),
            in_specs=[pl.BlockSpec((None, BLK, 256), lambda b, qb=qb: (b, qb, _DST["m_q"][0] // 256)),
                      pl.BlockSpec((None, S, 256), lambda b: (b, 0, _DST["m_k"][0] // 256)),
                      pl.BlockSpec((None, S, 256), lambda b: (b, 0, _DST["m_v"][0] // 256))],
            out_specs=pl.BlockSpec((None, BLK, 256), lambda b: (b, 0, 0)),
            out_shape=jax.ShapeDtypeStruct((B, BLK, 256), F32),
            compiler_params=_cparams(("parallel",)),
            name=f"moba_{qb}",
        )(y3, y3, y3))
    return jnp.stack(outs, axis=1).reshape(B * L, MOBA_HEADS * HEAD_DIM)


INT_MIN = -2 ** 31


def _dsa_kernel(q_ref, k_ref, v_ref, iq_ref, ik_ref, misc_ref, o_ref, key_ref, sel_ref, *, qt, top):
    TQ = q_ref.shape[0]
    S = k_ref.shape[0]
    iq_t = iq_ref[...].T.astype(BF16)
    w_t = misc_ref[...].T[MISC_W0:MISC_W0 + IDX_HEADS, :] * IDX_HEADS ** -0.5
    ik = ik_ref[...]
    ik_lo = ik.astype(BF16)
    ik_hi = pltpu.roll(ik, IDX_DIM, 1).astype(BF16)
    score = jnp.zeros((S, TQ), F32)
    for p in range(IDX_HEADS // 2):
        q_pair = iq_t[p * LANES:(p + 1) * LANES, :]
        score = score + jnp.maximum(_dot(ik_lo, q_pair), 0.0) * w_t[2 * p:2 * p + 1, :]
        score = score + jnp.maximum(_dot(ik_hi, q_pair), 0.0) * w_t[2 * p + 1:2 * p + 2, :]
    krow = lax.broadcasted_iota(jnp.int32, (S, TQ), 0)
    qpos = lax.broadcasted_iota(jnp.int32, (S, TQ), 1) + qt * TQ
    causal = krow <= qpos
    score = jnp.where(score == 0.0, 0.0, score)
    score = jnp.where(causal, score, -jnp.inf)
    bits = pltpu.bitcast(score, jnp.int32)
    key_ref[...] = bits ^ ((bits >> 31) & 0x7FFFFFFF)

    def count_ge(t):
        return jnp.sum(jnp.where(key_ref[...] >= t, 1.0, 0.0), axis=0, keepdims=True)

    kf = float(top)
    prefix = jnp.where(count_ge(jnp.zeros((1, TQ), jnp.int32)) >= kf, 0, INT_MIN).astype(jnp.int32)

    def bit_step(i, prefix):
        cand = prefix | jnp.left_shift(jnp.int32(1), 30 - i)
        return jnp.where(count_ge(cand) >= kf, cand, prefix)

    thr = lax.fori_loop(0, 31, bit_step, prefix)
    keys = key_ref[...]
    ge = keys >= thr
    n_ge = jnp.sum(jnp.where(ge, 1.0, 0.0), axis=0, keepdims=True)
    sel_ref[...] = jnp.where(ge & causal, 0.0, -jnp.inf)

    @pl.when(jnp.max(n_ge) > kf)
    def _():
        n_gt = jnp.sum(jnp.where(keys > thr, 1.0, 0.0), axis=0, keepdims=True)
        r = lax.broadcasted_iota(jnp.int32, (TQ, TQ), 0)
        c = lax.broadcasted_iota(jnp.int32, (TQ, TQ), 1)
        before = jnp.where(c < r, 1.0, 0.0).astype(BF16)
        room = kf - n_gt
        seen = jnp.zeros_like(n_gt)
        for i in range(S // TQ):
            rows = slice(i * TQ, (i + 1) * TQ)
            k_i = key_ref[rows, :]
            eq_i = k_i == thr
            eq_f = jnp.where(eq_i, 1.0, 0.0)
            n_eq_before = seen + _dot(before, eq_f.astype(BF16))
            keep = (k_i > thr) | (eq_i & (n_eq_before < room))
            valid = (r + i * TQ) <= (c + qt * TQ)
            sel_ref[rows, :] = jnp.where(keep & valid, 0.0, -jnp.inf)
            seen = seen + jnp.sum(eq_f, axis=0, keepdims=True)

    q_t = q_ref[...].T
    feat_head = lax.broadcasted_iota(jnp.int32, q_t.shape, 0) // HEAD_DIM
    kb = k_ref[...].astype(BF16)
    v_t = v_ref[...].T.astype(BF16)
    outs = []
    for h in range(DSA_HEADS):
        qth = jnp.where(feat_head == h, q_t, 0.0).astype(BF16)
        s = _dot(kb, qth) + sel_ref[...]
        m = jnp.max(s, axis=0, keepdims=True)
        p = jnp.exp(s - m)
        den = jnp.sum(p, axis=0, keepdims=True)
        outs.append(_dot(v_t[h * HEAD_DIM:(h + 1) * HEAD_DIM, :], p.astype(BF16)) / den)
    o_ref[...] = jnp.concatenate(outs, axis=0).T


def _dsa(y3, tq=256):
    B, L, _ = y3.shape
    top = min(DSA_TOPK, L // 4)
    outs = []
    for qt in range(L // tq):
        S = (qt + 1) * tq
        outs.append(pl.pallas_call(
            functools.partial(_dsa_kernel, qt=qt, top=top),
            grid=(B,),
            in_specs=[pl.BlockSpec((None, tq, 256), lambda b, qt=qt: (b, qt, _DST["s_q"][0] // 256)),
                      pl.BlockSpec((None, S, 256), lambda b: (b, 0, _DST["s_k"][0] // 256)),
                      pl.BlockSpec((None, S, 256), lambda b: (b, 0, _DST["s_v"][0] // 256)),
                      pl.BlockSpec((None, tq, 512), lambda b, qt=qt: (b, qt, _DST["i_q"][0] // 512)),
                      pl.BlockSpec((None, S, LANES), lambda b: (b, 0, _DST["i_k"][0] // LANES)),
                      pl.BlockSpec((None, tq, LANES), lambda b, qt=qt: (b, qt, _DST["misc"][0] // LANES))],
            out_specs=pl.BlockSpec((None, tq, 256), lambda b: (b, 0, 0)),
            out_shape=jax.ShapeDtypeStruct((B, tq, 256), F32),
            scratch_shapes=[pltpu.VMEM((S, tq), jnp.int32), pltpu.VMEM((S, tq), F32)],
            compiler_params=_cparams(("parallel",)),
            name=f"dsa_{qt}",
        )(y3, y3, y3, y3, y3, y3))
    return jnp.stack(outs, axis=1).reshape(B * L, DSA_HEADS * HEAD_DIM)


def _out_proj_kernel(h_ref, a_ref, b_ref, c_ref, w_ref, g_ref, beta_ref, rw_ref, rb_ref, h1_ref, comb_ref, *, alpha):
    wa, wb = a_ref.shape[1], b_ref.shape[1]
    mix = _dot(a_ref[...].astype(BF16), w_ref[0:wa, :])
    mix = mix + _dot(b_ref[...].astype(BF16), w_ref[wa:wa + wb, :])
    mix = mix + _dot(c_ref[...].astype(BF16), w_ref[wa + wb:, :])
    h1 = _layer_norm(alpha * h_ref[...] + mix, g_ref[...], beta_ref[...])
    h1_ref[...] = h1
    logits_t = _dot_hi(h1, rw_ref[...]).T[0:N_EXPERTS, :]
    scores = 1.0 / (1.0 + jnp.exp(-logits_t))
    sel = scores + rb_ref[...]
    rows = [sel[e:e + 1, :] for e in range(N_EXPERTS)]
    best_score, best = None, None
    for g in range(N_GROUPS):
        r = rows[g * EXPERTS_PER_GROUP:(g + 1) * EXPERTS_PER_GROUP]
        gs = None
        for i in range(EXPERTS_PER_GROUP):
            for j in range(i + 1, EXPERTS_PER_GROUP):
                pair = r[i] + r[j]
                gs = pair if gs is None else jnp.maximum(gs, pair)
        if g == 0:
            best_score, best = gs, jnp.zeros(gs.shape, jnp.int32)
        else:
            better = gs > best_score
            best = jnp.where(better, g, best)
            best_score = jnp.where(better, gs, best_score)
    expert = lax.broadcasted_iota(jnp.int32, sel.shape, 0)
    in_group = (expert // EXPERTS_PER_GROUP) == best
    masked = jnp.where(in_group, sel, -jnp.inf)
    rank = jnp.zeros(sel.shape, F32)
    for e in range(N_EXPERTS):
        other = masked[e:e + 1, :]
        ahead = (other > masked) | ((other == masked) & (e < expert))
        rank = rank + jnp.where(ahead, 1.0, 0.0)
    chosen = jnp.where((rank < TOP_K) & in_group, scores, 0.0)
    comb_t = chosen / jnp.sum(chosen, axis=0, keepdims=True)
    pad = jnp.zeros((LANES - N_EXPERTS, comb_t.shape[1]), F32)
    comb_ref[...] = jnp.concatenate([comb_t, pad], axis=0).T


def _out_proj(h, out_a, out_b, out_c, w_out, ln_g, ln_b, router_w, router_bias, alpha, tm=512):
    T, D = h.shape
    rw = jnp.zeros((D, LANES), F32).at[:, :N_EXPERTS].set(router_w)
    rb = jnp.broadcast_to(router_bias.astype(F32).reshape(N_EXPERTS, 1), (N_EXPERTS, tm))
    row = lambda i: (i, 0)
    full = lambda i: (0, 0)
    return pl.pallas_call(
        functools.partial(_out_proj_kernel, alpha=alpha),
        grid=(T // tm,),
        in_specs=[pl.BlockSpec((tm, D), row),
                  pl.BlockSpec((tm, out_a.shape[1]), row),
                  pl.BlockSpec((tm, out_b.shape[1]), row),
                  pl.BlockSpec((tm, out_c.shape[1]), row),
                  pl.BlockSpec(w_out.shape, full),
                  pl.BlockSpec((1, D), full), pl.BlockSpec((1, D), full),
                  pl.BlockSpec((D, LANES), full), pl.BlockSpec((N_EXPERTS, tm), full)],
        out_specs=[pl.BlockSpec((tm, D), row), pl.BlockSpec((tm, LANES), row)],
        out_shape=[jax.ShapeDtypeStruct((T, D), F32), jax.ShapeDtypeStruct((T, LANES), F32)],
        compiler_params=_cparams(("parallel",)),
        name="out_proj",
    )(h, out_a, out_b, out_c, w_out.astype(BF16), ln_g.reshape(1, D), ln_b.reshape(1, D), rw, rb)


def _moe_kernel(x_ref, comb_ref, wg_ref, wu_ref, wd_ref, g_ref, beta_ref, o_ref, xb_ref, acc_ref, *, alpha):
    e = pl.program_id(1)

    @pl.when(e == 0)
    def _():
        xb_ref[...] = x_ref[...].astype(BF16)
        acc_ref[...] = jnp.zeros_like(acc_ref)

    xb = xb_ref[...]
    act = _silu(_dot(xb, wg_ref[...])) * _dot(xb, wu_ref[...])
    comb = comb_ref[...]
    lane = lax.broadcasted_iota(jnp.int32, comb.shape, 1)
    w_e = jnp.sum(jnp.where(lane == e, comb, 0.0), axis=1, keepdims=True)
    acc_ref[...] += w_e * _dot(act.astype(BF16), wd_ref[...])

    @pl.when(e == pl.num_programs(1) - 1)
    def _():
        o_ref[...] = _layer_norm(alpha * x_ref[...] + acc_ref[...], g_ref[...], beta_ref[...])


def _moe(x, comb, w_gate, w_up, w_down, ln_g, ln_b, alpha, tm=1024):
    T, D = x.shape
    E, _, F = w_gate.shape
    return pl.pallas_call(
        functools.partial(_moe_kernel, alpha=alpha),
        grid=(T // tm, E),
        in_specs=[pl.BlockSpec((tm, D), lambda i, e: (i, 0)),
                  pl.BlockSpec((tm, LANES), lambda i, e: (i, 0)),
                  pl.BlockSpec((None, D, F), lambda i, e: (e, 0, 0)),
                  pl.BlockSpec((None, D, F), lambda i, e: (e, 0, 0)),
                  pl.BlockSpec((None, F, D), lambda i, e: (e, 0, 0)),
                  pl.BlockSpec((1, D), lambda i, e: (0, 0)),
                  pl.BlockSpec((1, D), lambda i, e: (0, 0))],
        out_specs=pl.BlockSpec((tm, D), lambda i, e: (i, 0)),
        out_shape=jax.ShapeDtypeStruct((T, D), F32),
        scratch_shapes=[pltpu.VMEM((tm, D), BF16), pltpu.VMEM((tm, D), F32)],
        compiler_params=_cparams(("parallel", "arbitrary")),
        name="moe",
    )(x, comb, w_gate.astype(BF16), w_up.astype(BF16), w_down.astype(BF16),
      ln_g.reshape(1, D), ln_b.reshape(1, D))


def kernel(x, positions, w_in, gla_gate_up, gla_gate_bias, gla_norm_gain, w_out, ln_mix_g, ln_mix_b, router_w, router_bias, w_expert_gate, w_expert_up, w_expert_down, ln_ffn_g, ln_ffn_b):
    B, L, D = x.shape
    depth = w_in.shape[0]
    alpha = (2 * depth) ** 0.25
    T = B * L
    tabs = _rope_tables(positions)
    h = x.reshape(T, D)
    for layer in range(depth):
        y = _in_proj(h, _pack_w_in(w_in[layer]), tabs)
        out_a = _gla(y, B, L, gla_gate_up[layer], gla_gate_bias[layer], gla_norm_gain[layer])
        y3 = y.reshape(B, L, NP)
        out_b = _moba(y3)
        out_c = _dsa(y3)
        h1, comb = _out_proj(h, out_a, out_b, out_c, w_out[layer], ln_mix_g[layer], ln_mix_b[layer],
                             router_w, router_bias, alpha)
        h = _moe(h1, comb, w_expert_gate[layer], w_expert_up[layer], w_expert_down[layer],
                 ln_ffn_g[layer], ln_ffn_b[layer], alpha)
    return h.reshape(B, L, D)
```

```python
import functools

import jax
import jax.numpy as jnp
from jax import lax
from jax.experimental import pallas as pl
from jax.experimental.pallas import tpu as pltpu

F32 = jnp.float32
BF16 = jnp.bfloat16
HIGHEST = lax.Precision.HIGHEST

HEAD_DIM = 64
GLA_HEADS, GLA_DK, GLA_DV = 4, 64, 128
GLA_GATE_RANK = 16
GLA_GATE_NORM = 16.0
GLA_CHUNK = 64
MOBA_HEADS, MOBA_BLOCK, MOBA_TOPK = 4, 256, 3
DSA_HEADS, DSA_TOPK = 4, 256
IDX_HEADS, IDX_DIM = 8, 64
ROPE_THETA = 500000.0
ROPE_HALF = 8
N_EXPERTS, N_GROUPS, TOP_K = 16, 4, 2
EXPERTS_PER_GROUP = N_EXPERTS // N_GROUPS
_PAIRS = tuple((i, j) for i in range(EXPERTS_PER_GROUP) for j in range(i + 1, EXPERTS_PER_GROUP))
N_CLASSES = N_GROUPS * len(_PAIRS)
CLASS_ROWS = 32
MOE_TILE = 256
LN_EPS = 1e-5
RMS_EPS = 1e-6

LANES = 128
VMEM_LIMIT = 56 * 1024 * 1024

_SRC = dict(g_q=(0, 256), g_k=(256, 256), g_v=(512, 512), g_rank=(1024, 16), g_gate=(1040, 512),
            m_q=(1552, 256), m_k=(1808, 256), m_v=(2064, 256),
            s_q=(2320, 256), s_k=(2576, 256), s_v=(2832, 256),
            i_q=(3088, 512), i_k=(3600, 64), i_w=(3664, 8))
_DST = dict(g_q=(0, 256), g_k=(256, 256), g_v=(512, 512), g_gate=(1024, 512),
            m_q=(1536, 256), m_k=(1792, 256), m_v=(2048, 256),
            s_q=(2304, 256), s_k=(2560, 256), s_v=(2816, 256),
            i_q=(3072, 512), i_k=(3584, 128), misc=(3712, 128))
NP = 3840
MISC_RANK0 = 0
MISC_W0 = 16
_ROPED = ("m_q", "m_k", "s_q", "s_k", "i_q", "i_k")
_SCALED = dict(g_q=GLA_DK ** -0.5, m_q=HEAD_DIM ** -0.5, s_q=HEAD_DIM ** -0.5, i_q=IDX_DIM ** -0.5)


def _cparams(sem):
    return pltpu.CompilerParams(dimension_semantics=sem, vmem_limit_bytes=VMEM_LIMIT)


def _dot(a, b):
    return jnp.dot(a, b, preferred_element_type=F32)


def _dot_hi(a, b):
    return jnp.dot(a, b, precision=HIGHEST, preferred_element_type=F32)


def _dot_nt(a, b):
    return lax.dot_general(a, b, (((1,), (1,)), ((), ())), preferred_element_type=F32)


def _dot_tn(a, b):
    return lax.dot_general(a, b, (((0,), (0,)), ((), ())), preferred_element_type=F32)


def _layer_norm(x, g, b):
    mu = jnp.mean(x, -1, keepdims=True)
    xc = x - mu
    var = jnp.mean(xc * xc, -1, keepdims=True)
    return xc * lax.rsqrt(var + LN_EPS) * g + b


def _silu(x):
    return x * (1.0 / (1.0 + jnp.exp(-x)))


def _rope_table_kernel(pos_ref, inv_ref, c_ref, s1_ref, s2_ref):
    ang = pos_ref[...].astype(F32) * inv_ref[...]
    lane = lax.broadcasted_iota(jnp.int32, ang.shape, 1) % HEAD_DIM
    cos, sin = jnp.cos(ang), jnp.sin(ang)
    first = lane < ROPE_HALF
    second = (lane >= ROPE_HALF) & (lane < 2 * ROPE_HALF)
    c_ref[...] = jnp.where(first | second, cos, 1.0)
    s1_ref[...] = jnp.where(first, -sin, 0.0)
    s2_ref[...] = jnp.where(second, sin, 0.0)


def _rope_tables(positions, tm=512):
    T = positions.size
    pos = positions.reshape(T, 1)
    j = (jnp.arange(LANES) % ROPE_HALF).astype(F32)
    inv = (ROPE_THETA ** (-j / ROPE_HALF)).reshape(1, LANES)
    out = jax.ShapeDtypeStruct((T, LANES), F32)
    return pl.pallas_call(
        _rope_table_kernel,
        grid=(T // tm,),
        in_specs=[pl.BlockSpec((tm, 1), lambda i: (i, 0)), pl.BlockSpec((1, LANES), lambda i: (0, 0))],
        out_specs=[pl.BlockSpec((tm, LANES), lambda i: (i, 0))] * 3,
        out_shape=[out, out, out],
        compiler_params=_cparams(("parallel",)),
        name="rope_tables",
    )(pos, inv)


def _in_proj_kernel(h_ref, w_ref, c_ref, s1_ref, s2_ref, y_ref):
    hb = h_ref[...].astype(BF16)
    c, s1, s2 = c_ref[...], s1_ref[...], s2_ref[...]
    for name, (start, width) in _DST.items():
        y = _dot(hb, w_ref[:, start:start + width])
        if name in _SCALED:
            y = y * _SCALED[name]
        if name in _ROPED:
            blocks = []
            for o in range(0, width, LANES):
                yb = y[:, o:o + LANES]
                up = pltpu.roll(yb, LANES - ROPE_HALF, 1)
                down = pltpu.roll(yb, ROPE_HALF, 1)
                blocks.append(yb * c + up * s1 + down * s2)
            y = blocks[0] if len(blocks) == 1 else jnp.concatenate(blocks, axis=1)
        y_ref[:, start:start + width] = y


def _pack_w_in(w_in):
    D = w_in.shape[0]
    cols = []
    for name, (start, width) in _DST.items():
        if name == "misc":
            r0, rw = _SRC["g_rank"]
            w0, ww = _SRC["i_w"]
            blk = jnp.zeros((D, width), w_in.dtype)
            blk = blk.at[:, MISC_RANK0:MISC_RANK0 + rw].set(w_in[:, r0:r0 + rw])
            blk = blk.at[:, MISC_W0:MISC_W0 + ww].set(w_in[:, w0:w0 + ww])
        else:
            s0, sw = _SRC[name]
            blk = w_in[:, s0:s0 + sw]
            if sw < width:
                blk = jnp.pad(blk, ((0, 0), (0, width - sw)))
        cols.append(blk)
    return jnp.concatenate(cols, axis=1).astype(BF16)


def _in_proj(h, w_packed, tabs, tm=512):
    T, D = h.shape
    tab_spec = pl.BlockSpec((tm, LANES), lambda i: (i, 0))
    return pl.pallas_call(
        _in_proj_kernel,
        grid=(T // tm,),
        in_specs=[pl.BlockSpec((tm, D), lambda i: (i, 0)),
                  pl.BlockSpec((D, NP), lambda i: (0, 0)),
                  tab_spec, tab_spec, tab_spec],
        out_specs=pl.BlockSpec((tm, NP), lambda i: (i, 0)),
        out_shape=jax.ShapeDtypeStruct((T, NP), F32),
        compiler_params=_cparams(("parallel",)),
        name="in_proj",
    )(h, w_packed, *tabs)


def _gla_kernel(q_ref, k_ref, v_ref, g_ref, misc_ref, up_ref, bias_ref, gain_ref, o_ref, st_ref, *, tl):
    @pl.when(pl.program_id(1) == 0)
    def _():
        st_ref[...] = jnp.zeros_like(st_ref)

    C = GLA_CHUNK
    dkw, dvw = GLA_HEADS * GLA_DK, GLA_HEADS * GLA_DV
    z = _dot_hi(misc_ref[...], up_ref[...]) + bias_ref[...]
    log_a = (jnp.minimum(z, 0.0) - jnp.log(1.0 + jnp.exp(-jnp.abs(z)))) / GLA_GATE_NORM
    ri = lax.broadcasted_iota(jnp.int32, (C, C), 0)
    ci = lax.broadcasted_iota(jnp.int32, (C, C), 1)
    causal = ci <= ri
    tri = jnp.where(causal, 1.0, 0.0)
    lane_head = lax.broadcasted_iota(jnp.int32, (C, dkw), 1) // GLA_DK
    sr = lax.broadcasted_iota(jnp.int32, (dvw, dkw), 0) // GLA_DV
    sc = lax.broadcasted_iota(jnp.int32, (dvw, dkw), 1) // GLA_DK
    diag = sr == sc
    gain = gain_ref[...]

    for c in range(tl // C):
        rows = slice(c * C, (c + 1) * C)
        b = _dot_hi(tri, log_a[rows])
        b_last = b[C - 1:C, :]
        q_dec = q_ref[rows, :] * jnp.exp(b)
        k_c = k_ref[rows, :]
        k_inv = (k_c * jnp.exp(-b)).astype(BF16)
        k_dec = (k_c * jnp.exp(b_last - b)).astype(BF16)
        v_c = v_ref[rows, :].astype(BF16)
        q_dec_b = q_dec.astype(BF16)
        st = st_ref[...]
        o = _dot_nt(q_dec_b, st.astype(BF16))
        intra = []
        for h in range(GLA_HEADS):
            qh = jnp.where(lane_head == h, q_dec, 0.0).astype(BF16)
            a = jnp.where(causal, _dot_nt(qh, k_inv), 0.0)
            intra.append(_dot(a.astype(BF16), v_c[:, h * GLA_DV:(h + 1) * GLA_DV]))
        o = o + jnp.concatenate(intra, axis=1)
        kv_t = _dot_tn(v_c, k_dec)
        st_ref[...] = jnp.exp(b_last) * st + jnp.where(diag, kv_t, 0.0)
        outs = []
        for h in range(GLA_HEADS):
            oh = o[:, h * GLA_DV:(h + 1) * GLA_DV]
            ms = jnp.mean(oh * oh, -1, keepdims=True)
            outs.append(oh * lax.rsqrt(ms + RMS_EPS) * gain)
        o_ref[rows, :] = jnp.concatenate(outs, axis=1) * _silu(g_ref[rows, :])


def _gla(y, B, L, gate_up, gate_bias, norm_gain, tl=512):
    T = B * L
    nt = L // tl
    up = jnp.zeros((LANES, GLA_HEADS * GLA_DK), F32).at[MISC_RANK0:MISC_RANK0 + GLA_GATE_RANK].set(gate_up)
    row = lambda b, t: b * nt + t
    return pl.pallas_call(
        functools.partial(_gla_kernel, tl=tl),
        grid=(B, nt),
        in_specs=[pl.BlockSpec((tl, 256), lambda b, t: (row(b, t), _DST["g_q"][0] // 256)),
                  pl.BlockSpec((tl, 256), lambda b, t: (row(b, t), _DST["g_k"][0] // 256)),
                  pl.BlockSpec((tl, 512), lambda b, t: (row(b, t), _DST["g_v"][0] // 512)),
                  pl.BlockSpec((tl, 512), lambda b, t: (row(b, t), _DST["g_gate"][0] // 512)),
                  pl.BlockSpec((tl, LANES), lambda b, t: (row(b, t), _DST["misc"][0] // LANES)),
                  pl.BlockSpec((LANES, 256), lambda b, t: (0, 0)),
                  pl.BlockSpec((1, 256), lambda b, t: (0, 0)),
                  pl.BlockSpec((1, GLA_DV), lambda b, t: (0, 0))],
        out_specs=pl.BlockSpec((tl, 512), lambda b, t: (row(b, t), 0)),
        out_shape=jax.ShapeDtypeStruct((T, GLA_HEADS * GLA_DV), F32),
        scratch_shapes=[pltpu.VMEM((GLA_HEADS * GLA_DV, GLA_HEADS * GLA_DK), F32)],
        compiler_params=_cparams(("parallel", "arbitrary")),
        name="gla",
    )(y, y, y, y, y, up, gate_bias.reshape(1, -1), norm_gain.reshape(1, -1))


def _moba_kernel(q_ref, k_ref, v_ref, o_ref, *, qb):
    BLK = MOBA_BLOCK
    qt = q_ref[...].T
    feat_head = lax.broadcasted_iota(jnp.int32, qt.shape, 0) // HEAD_DIM
    krow = lax.broadcasted_iota(jnp.int32, (BLK, BLK), 0)
    qcol = lax.broadcasted_iota(jnp.int32, (BLK, BLK), 1)
    own_bias = jnp.where(krow <= qcol, 0.0, -jnp.inf)
    k_blocks = [k_ref[j * BLK:(j + 1) * BLK, :] for j in range(qb + 1)]
    kb_blocks = [kb.astype(BF16) for kb in k_blocks]
    vt_blocks = [v_ref[j * BLK:(j + 1) * BLK, :].T.astype(BF16) for j in range(qb + 1)]
    if qb > 0:
        k_mean = jnp.concatenate([jnp.mean(kb, axis=0, keepdims=True) for kb in k_blocks[:qb]], axis=0)
    outs = []
    for h in range(MOBA_HEADS):
        qth = jnp.where(feat_head == h, qt, 0.0)
        qth_b = qth.astype(BF16)
        biases = []
        if qb > 0:
            gate = _dot_hi(k_mean, qth)
            g = [gate[j:j + 1, :] for j in range(qb)]
            for j in range(qb):
                rank = jnp.zeros_like(g[j])
                for i in range(qb):
                    if i == j:
                        continue
                    ahead = (g[i] >= g[j]) if i < j else (g[i] > g[j])
                    rank = rank + jnp.where(ahead, 1.0, 0.0)
                biases.append(jnp.where(rank < MOBA_TOPK, 0.0, -jnp.inf))
        s = [_dot(kb_blocks[j], qth_b) + biases[j] for j in range(qb)]
        s.append(_dot(kb_blocks[qb], qth_b) + own_bias)
        m = jnp.max(s[0], axis=0, keepdims=True)
        for sj in s[1:]:
            m = jnp.maximum(m, jnp.max(sj, axis=0, keepdims=True))
        den = jnp.zeros_like(m)
        acc = jnp.zeros((HEAD_DIM, BLK), F32)
        for j in range(qb + 1):
            p = jnp.exp(s[j] - m)
            den = den + jnp.sum(p, axis=0, keepdims=True)
            acc = acc + _dot(vt_blocks[j][h * HEAD_DIM:(h + 1) * HEAD_DIM, :], p.astype(BF16))
        outs.append(acc / den)
    o_ref[...] = jnp.concatenate(outs, axis=0).T


def _moba(y3):
    B, L, _ = y3.shape
    BLK = MOBA_BLOCK
    nb = L // BLK
    outs = []
    for qb in range(nb):
        S = (qb + 1) * BLK
        outs.append(pl.pallas_call(
            functools.partial(_moba_kernel, qb=qb),
            grid=(B,),
            in_specs=[pl.BlockSpec((None, BLK, 256), lambda b, qb=qb: (b, qb, _DST["m_q"][0] // 256)),
                      pl.BlockSpec((None, S, 256), lambda b: (b, 0, _DST["m_k"][0] // 256)),
                      pl.BlockSpec((None, S, 256), lambda b: (b, 0, _DST["m_v"][0] // 256))],
            out_specs=pl.BlockSpec((None, BLK, 256), lambda b: (b, 0, 0)),
            out_shape=jax.ShapeDtypeStruct((B, BLK, 256), F32),
            compiler_params=_cparams(("parallel",)),
            name=f"moba_{qb}",
        )(y3, y3, y3))
    return jnp.stack(outs, axis=1).reshape(B * L, MOBA_HEADS * HEAD_DIM)


INT_MIN = -2 ** 31


def _dsa_kernel(q_ref, k_ref, v_ref, iq_ref, ik_ref, misc_ref, o_ref, key_ref, sel_ref, *, qt, top):
    TQ = q_ref.shape[0]
    S = k_ref.shape[0]
    iq_t = iq_ref[...].T.astype(BF16)
    w_t = misc_ref[...].T[MISC_W0:MISC_W0 + IDX_HEADS, :] * IDX_HEADS ** -0.5
    ik = ik_ref[...]
    ik_lo = ik.astype(BF16)
    ik_hi = pltpu.roll(ik, IDX_DIM, 1).astype(BF16)
    score = jnp.zeros((S, TQ), F32)
    for p in range(IDX_HEADS // 2):
        q_pair = iq_t[p * LANES:(p + 1) * LANES, :]
        score = score + jnp.maximum(_dot(ik_lo, q_pair), 0.0) * w_t[2 * p:2 * p + 1, :]
        score = score + jnp.maximum(_dot(ik_hi, q_pair), 0.0) * w_t[2 * p + 1:2 * p + 2, :]
    krow = lax.broadcasted_iota(jnp.int32, (S, TQ), 0)
    qpos = lax.broadcasted_iota(jnp.int32, (S, TQ), 1) + qt * TQ
    causal = krow <= qpos
    score = jnp.where(score == 0.0, 0.0, score)
    score = jnp.where(causal, score, -jnp.inf)
    bits = pltpu.bitcast(score, jnp.int32)
    key_ref[...] = bits ^ ((bits >> 31) & 0x7FFFFFFF)

    def count_ge(t):
        return jnp.sum(jnp.where(key_ref[...] >= t, 1.0, 0.0), axis=0, keepdims=True)

    kf = float(top)
    prefix = jnp.where(count_ge(jnp.zeros((1, TQ), jnp.int32)) >= kf, 0, INT_MIN).astype(jnp.int32)

    def bit_step(i, prefix):
        cand = prefix | jnp.left_shift(jnp.int32(1), 30 - i)
        return jnp.where(count_ge(cand) >= kf, cand, prefix)

    thr = lax.fori_loop(0, 31, bit_step, prefix)
    keys = key_ref[...]
    ge = keys >= thr
    n_ge = jnp.sum(jnp.where(ge, 1.0, 0.0), axis=0, keepdims=True)
    sel_ref[...] = jnp.where(ge & causal, 0.0, -jnp.inf)

    @pl.when(jnp.max(n_ge) > kf)
    def _():
        n_gt = jnp.sum(jnp.where(keys > thr, 1.0, 0.0), axis=0, keepdims=True)
        r = lax.broadcasted_iota(jnp.int32, (TQ, TQ), 0)
        c = lax.broadcasted_iota(jnp.int32, (TQ, TQ), 1)
        before = jnp.where(c < r, 1.0, 0.0).astype(BF16)
        room = kf - n_gt
        seen = jnp.zeros_like(n_gt)
        for i in range(S // TQ):
            rows = slice(i * TQ, (i + 1) * TQ)
            k_i = key_ref[rows, :]
            eq_i = k_i == thr
            eq_f = jnp.where(eq_i, 1.0, 0.0)
            n_eq_before = seen + _dot(before, eq_f.astype(BF16))
            keep = (k_i > thr) | (eq_i & (n_eq_before < room))
            valid = (r + i * TQ) <= (c + qt * TQ)
            sel_ref[rows, :] = jnp.where(keep & valid, 0.0, -jnp.inf)
            seen = seen + jnp.sum(eq_f, axis=0, keepdims=True)

    q_t = q_ref[...].T
    feat_head = lax.broadcasted_iota(jnp.int32, q_t.shape, 0) // HEAD_DIM
    kb = k_ref[...].astype(BF16)
    v_t = v_ref[...].T.astype(BF16)
    outs = []
    for h in range(DSA_HEADS):
        qth = jnp.where(feat_head == h, q_t, 0.0).astype(BF16)
        s = _dot(kb, qth) + sel_ref[...]
        m = jnp.max(s, axis=0, keepdims=True)
        p = jnp.exp(s - m)
        den = jnp.sum(p, axis=0, keepdims=True)
        outs.append(_dot(v_t[h * HEAD_DIM:(h + 1) * HEAD_DIM, :], p.astype(BF16)) / den)
    o_ref[...] = jnp.concatenate(outs, axis=0).T


def _dsa(y3, tq=256):
    B, L, _ = y3.shape
    top = min(DSA_TOPK, L // 4)
    outs = []
    for qt in range(L // tq):
        S = (qt + 1) * tq
        outs.append(pl.pallas_call(
            functools.partial(_dsa_kernel, qt=qt, top=top),
            grid=(B,),
            in_specs=[pl.BlockSpec((None, tq, 256), lambda b, qt=qt: (b, qt, _DST["s_q"][0] // 256)),
                      pl.BlockSpec((None, S, 256), lambda b: (b, 0, _DST["s_k"][0] // 256)),
                      pl.BlockSpec((None, S, 256), lambda b: (b, 0, _DST["s_v"][0] // 256)),
                      pl.BlockSpec((None, tq, 512), lambda b, qt=qt: (b, qt, _DST["i_q"][0] // 512)),
                      pl.BlockSpec((None, S, LANES), lambda b: (b, 0, _DST["i_k"][0] // LANES)),
                      pl.BlockSpec((None, tq, LANES), lambda b, qt=qt: (b, qt, _DST["misc"][0] // LANES))],
            out_specs=pl.BlockSpec((None, tq, 256), lambda b: (b, 0, 0)),
            out_shape=jax.ShapeDtypeStruct((B, tq, 256), F32),
            scratch_shapes=[pltpu.VMEM((S, tq), jnp.int32), pltpu.VMEM((S, tq), F32)],
            compiler_params=_cparams(("parallel",)),
            name=f"dsa_{qt}",
        )(y3, y3, y3, y3, y3, y3))
    return jnp.stack(outs, axis=1).reshape(B * L, DSA_HEADS * HEAD_DIM)


def _out_proj_kernel(h_ref, a_ref, b_ref, c_ref, w_ref, g_ref, beta_ref, rwh_ref, rwl_ref, rb_ref,
                     h1_ref, cls_ref, rank_ref, cnt_ref, carry_ref, *, alpha):
    @pl.when(pl.program_id(0) == 0)
    def _():
        carry_ref[...] = jnp.zeros_like(carry_ref)

    D = h_ref.shape[1]
    tm = h_ref.shape[0]
    wa, wb = a_ref.shape[1], b_ref.shape[1]
    mix = _dot(a_ref[...].astype(BF16), w_ref[0:wa, :])
    mix = mix + _dot(b_ref[...].astype(BF16), w_ref[wa:wa + wb, :])
    mix = mix + _dot(c_ref[...].astype(BF16), w_ref[wa + wb:, :])
    h1 = _layer_norm(alpha * h_ref[...] + mix, g_ref[...], beta_ref[...])
    h1_ref[:, 0:D] = h1
    h1_hi = h1.astype(BF16)
    h1_lo = (h1 - h1_hi.astype(F32)).astype(BF16)
    logits = _dot(h1_hi, rwh_ref[...]) + _dot(h1_lo, rwh_ref[...]) + _dot(h1_hi, rwl_ref[...])
    logits_t = logits.T[0:N_EXPERTS, :]
    scores = 1.0 / (1.0 + jnp.exp(-logits_t))
    sel = scores + rb_ref[...]
    rows = [sel[e:e + 1, :] for e in range(N_EXPERTS)]
    best_score, best = None, None
    for g in range(N_GROUPS):
        r = rows[g * EXPERTS_PER_GROUP:(g + 1) * EXPERTS_PER_GROUP]
        gs = None
        for i in range(EXPERTS_PER_GROUP):
            for j in range(i + 1, EXPERTS_PER_GROUP):
                pair = r[i] + r[j]
                gs = pair if gs is None else jnp.maximum(gs, pair)
        if g == 0:
            best_score, best = gs, jnp.zeros(gs.shape, jnp.int32)
        else:
            better = gs > best_score
            best = jnp.where(better, g, best)
            best_score = jnp.where(better, gs, best_score)
    expert = lax.broadcasted_iota(jnp.int32, sel.shape, 0)
    in_group = (expert // EXPERTS_PER_GROUP) == best
    masked = jnp.where(in_group, sel, -jnp.inf)
    rank = jnp.zeros(sel.shape, F32)
    for e in range(N_EXPERTS):
        other = masked[e:e + 1, :]
        ahead = (other > masked) | ((other == masked) & (e < expert))
        rank = rank + jnp.where(ahead, 1.0, 0.0)
    picked = jnp.where((rank < TOP_K) & in_group, 1.0, 0.0)
    chosen = picked * scores
    comb_t = chosen / jnp.sum(chosen, axis=0, keepdims=True)
    G = EXPERTS_PER_GROUP
    m = [sum(picked[g * G + i:g * G + i + 1, :] for g in range(N_GROUPS)) for i in range(G)]
    cw = [sum(comb_t[g * G + i:g * G + i + 1, :] for g in range(N_GROUPS)) for i in range(G)]
    pair = jnp.zeros_like(m[0])
    for idx, (i, j) in enumerate(_PAIRS):
        pair = pair + float(idx) * m[i] * m[j]
    cls = best * len(_PAIRS) + pair.astype(jnp.int32)
    w_lo = jnp.where(m[0] > 0, cw[0], jnp.where(m[1] > 0, cw[1], cw[2]))
    w_hi = jnp.where(m[3] > 0, cw[3], jnp.where(m[2] > 0, cw[2], cw[1]))
    pad = jnp.zeros((LANES - 2, tm), F32)
    h1_ref[:, D:D + LANES] = jnp.concatenate([w_lo, w_hi, pad], axis=0).T
    onehot = jnp.where(lax.broadcasted_iota(jnp.int32, (CLASS_ROWS, tm), 0) == cls, 1.0, 0.0)
    r = lax.broadcasted_iota(jnp.int32, (tm, tm), 0)
    c = lax.broadcasted_iota(jnp.int32, (tm, tm), 1)
    earlier = jnp.where(r < c, 1.0, 0.0).astype(BF16)
    before = _dot(onehot.astype(BF16), earlier) + carry_ref[...]
    cls_ref[...] = cls
    rank_ref[...] = jnp.sum(onehot * before, axis=0, keepdims=True).astype(jnp.int32)
    carry = carry_ref[...] + jnp.sum(onehot, axis=1, keepdims=True)
    carry_ref[...] = carry
    cnt_ref[...] = carry[:, 0:LANES]


def _out_proj(h, out_a, out_b, out_c, w_out, ln_g, ln_b, router_w, router_bias, alpha, tm=512):
    T, D = h.shape
    nt = T // tm
    rw = jnp.zeros((D, LANES), F32).at[:, :N_EXPERTS].set(router_w)
    rw_hi = rw.astype(BF16)
    rw_lo = (rw - rw_hi.astype(F32)).astype(BF16)
    rb = jnp.broadcast_to(router_bias.astype(F32).reshape(N_EXPERTS, 1), (N_EXPERTS, tm))
    row = lambda i: (i, 0)
    full = lambda i: (0, 0)
    vec = pl.BlockSpec((None, 1, tm), lambda i: (i, 0, 0))
    return pl.pallas_call(
        functools.partial(_out_proj_kernel, alpha=alpha),
        grid=(nt,),
        in_specs=[pl.BlockSpec((tm, D), row),
                  pl.BlockSpec((tm, out_a.shape[1]), row),
                  pl.BlockSpec((tm, out_b.shape[1]), row),
                  pl.BlockSpec((tm, out_c.shape[1]), row),
                  pl.BlockSpec(w_out.shape, full),
                  pl.BlockSpec((1, D), full), pl.BlockSpec((1, D), full),
                  pl.BlockSpec((D, LANES), full), pl.BlockSpec((D, LANES), full),
                  pl.BlockSpec((N_EXPERTS, tm), full)],
        out_specs=[pl.BlockSpec((tm, D + LANES), row), vec, vec, pl.BlockSpec((CLASS_ROWS, LANES), full)],
        out_shape=[jax.ShapeDtypeStruct((T, D + LANES), F32),
                   jax.ShapeDtypeStruct((nt, 1, tm), jnp.int32),
                   jax.ShapeDtypeStruct((nt, 1, tm), jnp.int32),
                   jax.ShapeDtypeStruct((CLASS_ROWS, LANES), F32)],
        scratch_shapes=[pltpu.VMEM((CLASS_ROWS, tm), F32)],
        compiler_params=_cparams(("arbitrary",)),
        name="out_proj",
    )(h, out_a, out_b, out_c, w_out.astype(BF16), ln_g.reshape(1, D), ln_b.reshape(1, D), rw_hi, rw_lo, rb)


def _plan(cls, rank, counts, tile):
    n_tiles = cls.size // tile + N_CLASSES
    cnt = counts[:N_CLASSES, 0].astype(jnp.int32)
    tiles_per_class = (cnt + tile - 1) // tile
    tile_end = jnp.cumsum(tiles_per_class)
    row_start = (tile_end - tiles_per_class) * tile
    pos = (row_start[cls.reshape(-1)] + rank.reshape(-1)).astype(jnp.int32)
    t = jnp.arange(n_tiles, dtype=jnp.int32)
    tile_cls = jnp.minimum(jnp.sum((t[:, None] >= tile_end[None, :]).astype(jnp.int32), axis=1), N_CLASSES - 1)
    used = (t < tile_end[-1]).astype(jnp.int32)
    lo = jnp.array([p[0] for p in _PAIRS], jnp.int32)
    hi = jnp.array([p[1] for p in _PAIRS], jnp.int32)
    group = tile_cls // len(_PAIRS)
    e_lo = group * EXPERTS_PER_GROUP + lo[tile_cls % len(_PAIRS)]
    e_hi = group * EXPERTS_PER_GROUP + hi[tile_cls % len(_PAIRS)]
    return pos, e_lo, e_hi, used, n_tiles


def _row_copies(pos_hbm, pos_smem, idx_sem, n_rows, make_copy):
    base = pl.multiple_of(pl.program_id(0) * n_rows, n_rows)
    idx_copy = pltpu.make_async_copy(pos_hbm.at[pl.ds(base, n_rows)], pos_smem, idx_sem)
    idx_copy.start()
    idx_copy.wait()

    def issue(r, carry):
        make_copy(r, pos_smem[r]).start()
        return carry

    lax.fori_loop(0, n_rows, issue, 0, unroll=8)

    def drain(r, carry):
        make_copy(r, pos_smem[r]).wait()
        return carry

    lax.fori_loop(0, n_rows, drain, 0, unroll=8)


def _dispatch_kernel(pos_hbm, x_ref, zero_hbm, xs_hbm, pos_smem, idx_sem, sem):
    del zero_hbm
    n_rows = x_ref.shape[0]
    _row_copies(pos_hbm, pos_smem, idx_sem, n_rows,
                lambda r, p: pltpu.make_async_copy(x_ref.at[r], xs_hbm.at[p], sem))


def _dispatch(x_aug, pos, n_rows_sorted, tm=2048):
    T, W = x_aug.shape
    return pl.pallas_call(
        _dispatch_kernel,
        grid=(T // tm,),
        in_specs=[pl.BlockSpec(memory_space=pl.ANY),
                  pl.BlockSpec((tm, W), lambda i: (i, 0)),
                  pl.BlockSpec(memory_space=pl.ANY)],
        out_specs=pl.BlockSpec(memory_space=pl.ANY),
        out_shape=jax.ShapeDtypeStruct((n_rows_sorted, W), F32),
        scratch_shapes=[pltpu.SMEM((tm,), jnp.int32), pltpu.SemaphoreType.DMA(()), pltpu.SemaphoreType.DMA(())],
        input_output_aliases={2: 0},
        compiler_params=_cparams(("arbitrary",)),
        name="moe_dispatch",
    )(pos, x_aug, jnp.zeros((n_rows_sorted, W), F32))


def _combine_kernel(pos_hbm, ys_hbm, o_ref, pos_smem, idx_sem, sem):
    n_rows = o_ref.shape[0]
    _row_copies(pos_hbm, pos_smem, idx_sem, n_rows,
                lambda r, p: pltpu.make_async_copy(ys_hbm.at[p], o_ref.at[r], sem))


def _combine(ys, pos, tm=2048):
    T = pos.shape[0]
    D = ys.shape[1]
    return pl.pallas_call(
        _combine_kernel,
        grid=(T // tm,),
        in_specs=[pl.BlockSpec(memory_space=pl.ANY), pl.BlockSpec(memory_space=pl.ANY)],
        out_specs=pl.BlockSpec((tm, D), lambda i: (i, 0)),
        out_shape=jax.ShapeDtypeStruct((T, D), F32),
        scratch_shapes=[pltpu.SMEM((tm,), jnp.int32), pltpu.SemaphoreType.DMA(()), pltpu.SemaphoreType.DMA(())],
        compiler_params=_cparams(("arbitrary",)),
        name="moe_combine",
    )(pos, ys)


def _moe_kernel(e_lo_ref, e_hi_ref, used_ref, xs_ref, wg1_ref, wu1_ref, wd1_ref, wg2_ref, wu2_ref, wd2_ref,
                g_ref, beta_ref, ys_ref, *, alpha):
    del e_lo_ref, e_hi_ref
    D = ys_ref.shape[1]
    used = used_ref[pl.program_id(0)] > 0

    @pl.when(used)
    def _():
        x = xs_ref[:, 0:D]
        w = xs_ref[:, D:D + LANES]
        xb = x.astype(BF16)

        def expert(wg_ref, wu_ref, wd_ref):
            act = _silu(_dot(xb, wg_ref[...])) * _dot(xb, wu_ref[...])
            return _dot(act.astype(BF16), wd_ref[...])

        ffn = w[:, 0:1] * expert(wg1_ref, wu1_ref, wd1_ref) + w[:, 1:2] * expert(wg2_ref, wu2_ref, wd2_ref)
        ys_ref[...] = _layer_norm(alpha * x + ffn, g_ref[...], beta_ref[...])

    @pl.when(jnp.logical_not(used))
    def _():
        ys_ref[...] = jnp.zeros_like(ys_ref)


def _moe(xs, e_lo, e_hi, used, w_gate, w_up, w_down, ln_g, ln_b, alpha, tile):
    R, W = xs.shape
    E, D, F = w_gate.shape
    first = lambda i, lo, hi, u: (lo[i], 0, 0)
    second = lambda i, lo, hi, u: (hi[i], 0, 0)
    const = lambda i, lo, hi, u: (0, 0)
    wg, wu, wd = w_gate.astype(BF16), w_up.astype(BF16), w_down.astype(BF16)
    grid_spec = pltpu.PrefetchScalarGridSpec(
        num_scalar_prefetch=3,
        grid=(R // tile,),
        in_specs=[pl.BlockSpec((tile, W), lambda i, lo, hi, u: (i, 0)),
                  pl.BlockSpec((None, D, F), first), pl.BlockSpec((None, D, F), first),
                  pl.BlockSpec((None, F, D), first),
                  pl.BlockSpec((None, D, F), second), pl.BlockSpec((None, D, F), second),
                  pl.BlockSpec((None, F, D), second),
                  pl.BlockSpec((1, D), const), pl.BlockSpec((1, D), const)],
        out_specs=pl.BlockSpec((tile, D), lambda i, lo, hi, u: (i, 0)),
    )
    return pl.pallas_call(
        functools.partial(_moe_kernel, alpha=alpha),
        grid_spec=grid_spec,
        out_shape=jax.ShapeDtypeStruct((R, D), F32),
        compiler_params=_cparams(("arbitrary",)),
        name="moe",
    )(e_lo, e_hi, used, xs, wg, wu, wd, wg, wu, wd, ln_g.reshape(1, D), ln_b.reshape(1, D))


def _moe_block(h1_aug, cls, rank, counts, w_gate, w_up, w_down, ln_g, ln_b, alpha, tile=MOE_TILE):
    pos, e_lo, e_hi, used, n_tiles = _plan(cls, rank, counts, tile)
    xs = _dispatch(h1_aug, pos, n_tiles * tile)
    ys = _moe(xs, e_lo, e_hi, used, w_gate, w_up, w_down, ln_g, ln_b, alpha, tile)
    return _combine(ys, pos)


def kernel(x, positions, w_in, gla_gate_up, gla_gate_bias, gla_norm_gain, w_out, ln_mix_g, ln_mix_b, router_w, router_bias, w_expert_gate, w_expert_up, w_expert_down, ln_ffn_g, ln_ffn_b):
    B, L, D = x.shape
    depth = w_in.shape[0]
    alpha = (2 * depth) ** 0.25
    T = B * L
    tabs = _rope_tables(positions)
    h = x.reshape(T, D)
    for layer in range(depth):
        y = _in_proj(h, _pack_w_in(w_in[layer]), tabs)
        out_a = _gla(y, B, L, gla_gate_up[layer], gla_gate_bias[layer], gla_norm_gain[layer])
        y3 = y.reshape(B, L, NP)
        out_b = _moba(y3)
        out_c = _dsa(y3)
        h1_aug, cls, rank, counts = _out_proj(h, out_a, out_b, out_c, w_out[layer], ln_mix_g[layer],
                                              ln_mix_b[layer], router_w, router_bias, alpha)
        h = _moe_block(h1_aug, cls, rank, counts, w_expert_gate[layer], w_expert_up[layer],
                       w_expert_down[layer], ln_ffn_g[layer], ln_ffn_b[layer], alpha)
    return h.reshape(B, L, D)
```

```python
import functools

import jax
import jax.numpy as jnp
from jax import lax
from jax.experimental import pallas as pl
from jax.experimental.pallas import tpu as pltpu

F32 = jnp.float32
BF16 = jnp.bfloat16
HIGHEST = lax.Precision.HIGHEST

HEAD_DIM = 64
GLA_HEADS, GLA_DK, GLA_DV = 4, 64, 128
GLA_GATE_RANK = 16
GLA_GATE_NORM = 16.0
GLA_CHUNK = 64
MOBA_HEADS, MOBA_BLOCK, MOBA_TOPK = 4, 256, 3
DSA_HEADS, DSA_TOPK = 4, 256
IDX_HEADS, IDX_DIM = 8, 64
ROPE_THETA = 500000.0
ROPE_HALF = 8
N_EXPERTS, N_GROUPS, TOP_K = 16, 4, 2
EXPERTS_PER_GROUP = N_EXPERTS // N_GROUPS
_PAIRS = tuple((i, j) for i in range(EXPERTS_PER_GROUP) for j in range(i + 1, EXPERTS_PER_GROUP))
N_CLASSES = N_GROUPS * len(_PAIRS)
CLASS_ROWS = 32
MOE_TILE = 256
LN_EPS = 1e-5
RMS_EPS = 1e-6

LANES = 128
VMEM_LIMIT = 56 * 1024 * 1024

_SRC = dict(g_q=(0, 256), g_k=(256, 256), g_v=(512, 512), g_rank=(1024, 16), g_gate=(1040, 512),
            m_q=(1552, 256), m_k=(1808, 256), m_v=(2064, 256),
            s_q=(2320, 256), s_k=(2576, 256), s_v=(2832, 256),
            i_q=(3088, 512), i_k=(3600, 64), i_w=(3664, 8))
_DST = dict(g_q=(0, 256), g_k=(256, 256), g_v=(512, 512), g_gate=(1024, 512),
            m_q=(1536, 256), m_k=(1792, 256), m_v=(2048, 256),
            s_q=(2304, 256), s_k=(2560, 256), s_v=(2816, 256),
            i_q=(3072, 512), i_k=(3584, 128), misc=(3712, 128))
NP = 3840
MISC_RANK0 = 0
MISC_W0 = 16
_ROPED = ("m_q", "m_k", "s_q", "s_k", "i_q", "i_k")
_SCALED = dict(g_q=GLA_DK ** -0.5, m_q=HEAD_DIM ** -0.5, s_q=HEAD_DIM ** -0.5, i_q=IDX_DIM ** -0.5)


def _cparams(sem):
    return pltpu.CompilerParams(dimension_semantics=sem, vmem_limit_bytes=VMEM_LIMIT)


def _dot(a, b):
    return jnp.dot(a, b, preferred_element_type=F32)


def _dot_split(a, b):
    a_hi, a_lo = _split_bf16(a, 2)
    b_hi, b_lo = _split_bf16(b, 2)
    return _dot(a_hi, b_hi) + _dot(a_lo, b_hi) + _dot(a_hi, b_lo)


def _split_bf16(x, terms):
    parts = []
    for _ in range(terms - 1):
        p = x.astype(BF16)
        parts.append(p)
        x = x - p.astype(F32)
    parts.append(x.astype(BF16))
    return parts


def _dot_nt(a, b):
    return lax.dot_general(a, b, (((1,), (1,)), ((), ())), preferred_element_type=F32)


def _dot_tn(a, b):
    return lax.dot_general(a, b, (((0,), (0,)), ((), ())), preferred_element_type=F32)


def _layer_norm(x, g, b):
    mu = jnp.mean(x, -1, keepdims=True)
    xc = x - mu
    var = jnp.mean(xc * xc, -1, keepdims=True)
    return xc * lax.rsqrt(var + LN_EPS) * g + b


def _silu(x):
    return x * (1.0 / (1.0 + jnp.exp(-x)))


def _rope_table_kernel(pos_ref, inv_ref, c_ref, s1_ref, s2_ref):
    ang = pos_ref[...].astype(F32) * inv_ref[...]
    lane = lax.broadcasted_iota(jnp.int32, ang.shape, 1) % HEAD_DIM
    cos, sin = jnp.cos(ang), jnp.sin(ang)
    first = lane < ROPE_HALF
    second = (lane >= ROPE_HALF) & (lane < 2 * ROPE_HALF)
    c_ref[...] = jnp.where(first | second, cos, 1.0)
    s1_ref[...] = jnp.where(first, -sin, 0.0)
    s2_ref[...] = jnp.where(second, sin, 0.0)


def _rope_tables(positions, tm=512):
    T = positions.size
    pos = positions.reshape(T, 1)
    j = (jnp.arange(LANES) % ROPE_HALF).astype(F32)
    inv = (ROPE_THETA ** (-j / ROPE_HALF)).reshape(1, LANES)
    out = jax.ShapeDtypeStruct((T, LANES), F32)
    return pl.pallas_call(
        _rope_table_kernel,
        grid=(T // tm,),
        in_specs=[pl.BlockSpec((tm, 1), lambda i: (i, 0)), pl.BlockSpec((1, LANES), lambda i: (0, 0))],
        out_specs=[pl.BlockSpec((tm, LANES), lambda i: (i, 0))] * 3,
        out_shape=[out, out, out],
        compiler_params=_cparams(("parallel",)),
        name="rope_tables",
    )(pos, inv)


def _in_proj_kernel(h_ref, w_ref, c_ref, s1_ref, s2_ref, y_ref):
    hb = h_ref[...].astype(BF16)
    c, s1, s2 = c_ref[...], s1_ref[...], s2_ref[...]
    for name, (start, width) in _DST.items():
        y = _dot(hb, w_ref[:, start:start + width])
        if name in _SCALED:
            y = y * _SCALED[name]
        if name in _ROPED:
            blocks = []
            for o in range(0, width, LANES):
                yb = y[:, o:o + LANES]
                up = pltpu.roll(yb, LANES - ROPE_HALF, 1)
                down = pltpu.roll(yb, ROPE_HALF, 1)
                blocks.append(yb * c + up * s1 + down * s2)
            y = blocks[0] if len(blocks) == 1 else jnp.concatenate(blocks, axis=1)
        y_ref[:, start:start + width] = y


def _pack_w_in(w_in):
    D = w_in.shape[0]
    cols = []
    for name, (start, width) in _DST.items():
        if name == "misc":
            r0, rw = _SRC["g_rank"]
            w0, ww = _SRC["i_w"]
            blk = jnp.zeros((D, width), w_in.dtype)
            blk = blk.at[:, MISC_RANK0:MISC_RANK0 + rw].set(w_in[:, r0:r0 + rw])
            blk = blk.at[:, MISC_W0:MISC_W0 + ww].set(w_in[:, w0:w0 + ww])
        else:
            s0, sw = _SRC[name]
            blk = w_in[:, s0:s0 + sw]
            if sw < width:
                blk = jnp.pad(blk, ((0, 0), (0, width - sw)))
        cols.append(blk)
    return jnp.concatenate(cols, axis=1).astype(BF16)


def _in_proj(h, w_packed, tabs, tm=512):
    T, D = h.shape
    tab_spec = pl.BlockSpec((tm, LANES), lambda i: (i, 0))
    return pl.pallas_call(
        _in_proj_kernel,
        grid=(T // tm,),
        in_specs=[pl.BlockSpec((tm, D), lambda i: (i, 0)),
                  pl.BlockSpec((D, NP), lambda i: (0, 0)),
                  tab_spec, tab_spec, tab_spec],
        out_specs=pl.BlockSpec((tm, NP), lambda i: (i, 0)),
        out_shape=jax.ShapeDtypeStruct((T, NP), F32),
        compiler_params=_cparams(("parallel",)),
        name="in_proj",
    )(h, w_packed, *tabs)


def _gla_kernel(q_ref, k_ref, v_ref, g_ref, misc_ref, up_ref, bias_ref, gain_ref, o_ref, st_ref, *, tl):
    @pl.when(pl.program_id(1) == 0)
    def _():
        st_ref[...] = jnp.zeros_like(st_ref)

    C = GLA_CHUNK
    dkw, dvw = GLA_HEADS * GLA_DK, GLA_HEADS * GLA_DV
    z = _dot_split(misc_ref[...], up_ref[...]) + bias_ref[...]
    log_a = (jnp.minimum(z, 0.0) - jnp.log(1.0 + jnp.exp(-jnp.abs(z)))) / GLA_GATE_NORM
    log_a_terms = _split_bf16(log_a, 3)
    ri = lax.broadcasted_iota(jnp.int32, (C, C), 0)
    ci = lax.broadcasted_iota(jnp.int32, (C, C), 1)
    causal = ci <= ri
    tri = jnp.where(causal, 1.0, 0.0).astype(BF16)
    head_rows = lax.broadcasted_iota(jnp.int32, (GLA_HEADS * C, dkw), 0) // C
    head_lanes = lax.broadcasted_iota(jnp.int32, (GLA_HEADS * C, dkw), 1) // GLA_DK
    own_head = head_rows == head_lanes
    sr = lax.broadcasted_iota(jnp.int32, (dvw, dkw), 0) // GLA_DV
    sc = lax.broadcasted_iota(jnp.int32, (dvw, dkw), 1) // GLA_DK
    diag = sr == sc
    gain = gain_ref[...]

    for c in range(tl // C):
        rows = slice(c * C, (c + 1) * C)
        b = sum(_dot(tri, term[rows]) for term in log_a_terms)
        b_last = b[C - 1:C, :]
        q_dec = q_ref[rows, :] * jnp.exp(b)
        k_c = k_ref[rows, :]
        k_inv = (k_c * jnp.exp(-b)).astype(BF16)
        k_dec = (k_c * jnp.exp(b_last - b)).astype(BF16)
        v_c = v_ref[rows, :].astype(BF16)
        q_dec_b = q_dec.astype(BF16)
        st = st_ref[...]
        o = _dot_nt(q_dec_b, st.astype(BF16))
        q_heads = jnp.where(own_head, jnp.concatenate([q_dec] * GLA_HEADS, axis=0), 0.0).astype(BF16)
        a_all = _dot_nt(q_heads, k_inv)
        intra = []
        for h in range(GLA_HEADS):
            a = jnp.where(causal, a_all[h * C:(h + 1) * C, :], 0.0)
            intra.append(_dot(a.astype(BF16), v_c[:, h * GLA_DV:(h + 1) * GLA_DV]))
        o = o + jnp.concatenate(intra, axis=1)
        kv_t = _dot_tn(v_c, k_dec)
        st_ref[...] = jnp.exp(b_last) * st + jnp.where(diag, kv_t, 0.0)
        outs = []
        for h in range(GLA_HEADS):
            oh = o[:, h * GLA_DV:(h + 1) * GLA_DV]
            ms = jnp.mean(oh * oh, -1, keepdims=True)
            outs.append(oh * lax.rsqrt(ms + RMS_EPS) * gain)
        o_ref[rows, :] = jnp.concatenate(outs, axis=1) * _silu(g_ref[rows, :])


def _gla(y, B, L, gate_up, gate_bias, norm_gain, tl=512):
    T = B * L
    nt = L // tl
    up = jnp.zeros((LANES, GLA_HEADS * GLA_DK), F32).at[MISC_RANK0:MISC_RANK0 + GLA_GATE_RANK].set(gate_up)
    row = lambda b, t: b * nt + t
    return pl.pallas_call(
        functools.partial(_gla_kernel, tl=tl),
        grid=(B, nt),
        in_specs=[pl.BlockSpec((tl, 256), lambda b, t: (row(b, t), _DST["g_q"][0] // 256)),
                  pl.BlockSpec((tl, 256), lambda b, t: (row(b, t), _DST["g_k"][0] // 256)),
                  pl.BlockSpec((tl, 512), lambda b, t: (row(b, t), _DST["g_v"][0] // 512)),
                  pl.BlockSpec((tl, 512), lambda b, t: (row(b, t), _DST["g_gate"][0] // 512)),
                  pl.BlockSpec((tl, LANES), lambda b, t: (row(b, t), _DST["misc"][0] // LANES)),
                  pl.BlockSpec((LANES, 256), lambda b, t: (0, 0)),
                  pl.BlockSpec((1, 256), lambda b, t: (0, 0)),
                  pl.BlockSpec((1, GLA_DV), lambda b, t: (0, 0))],
        out_specs=pl.BlockSpec((tl, 512), lambda b, t: (row(b, t), 0)),
        out_shape=jax.ShapeDtypeStruct((T, GLA_HEADS * GLA_DV), F32),
        scratch_shapes=[pltpu.VMEM((GLA_HEADS * GLA_DV, GLA_HEADS * GLA_DK), F32)],
        compiler_params=_cparams(("parallel", "arbitrary")),
        name="gla",
    )(y, y, y, y, y, up, gate_bias.reshape(1, -1), norm_gain.reshape(1, -1))


def _moba_kernel(q_ref, k_ref, v_ref, o_ref, *, qb):
    BLK = MOBA_BLOCK
    qt = q_ref[...].T
    feat_head = lax.broadcasted_iota(jnp.int32, qt.shape, 0) // HEAD_DIM
    krow = lax.broadcasted_iota(jnp.int32, (BLK, BLK), 0)
    qcol = lax.broadcasted_iota(jnp.int32, (BLK, BLK), 1)
    own_bias = jnp.where(krow <= qcol, 0.0, -jnp.inf)
    k_blocks = [k_ref[j * BLK:(j + 1) * BLK, :] for j in range(qb + 1)]
    kb_blocks = [kb.astype(BF16) for kb in k_blocks]
    vt_blocks = [v_ref[j * BLK:(j + 1) * BLK, :].T.astype(BF16) for j in range(qb + 1)]
    if qb > 0:
        k_mean = jnp.concatenate([jnp.mean(kb, axis=0, keepdims=True) for kb in k_blocks[:qb]], axis=0)
    outs = []
    for h in range(MOBA_HEADS):
        qth = jnp.where(feat_head == h, qt, 0.0)
        qth_b = qth.astype(BF16)
        biases = []
        if qb > 0:
            gate = _dot_split(k_mean, qth)
            g = [gate[j:j + 1, :] for j in range(qb)]
            for j in range(qb):
                rank = jnp.zeros_like(g[j])
                for i in range(qb):
                    if i == j:
                        continue
                    ahead = (g[i] >= g[j]) if i < j else (g[i] > g[j])
                    rank = rank + jnp.where(ahead, 1.0, 0.0)
                biases.append(jnp.where(rank < MOBA_TOPK, 0.0, -jnp.inf))
        s = [_dot(kb_blocks[j], qth_b) + biases[j] for j in range(qb)]
        s.append(_dot(kb_blocks[qb], qth_b) + own_bias)
        m = jnp.max(s[0], axis=0, keepdims=True)
        for sj in s[1:]:
            m = jnp.maximum(m, jnp.max(sj, axis=0, keepdims=True))
        den = jnp.zeros_like(m)
        acc = jnp.zeros((HEAD_DIM, BLK), F32)
        for j in range(qb + 1):
            p = jnp.exp(s[j] - m)
            den = den + jnp.sum(p, axis=0, keepdims=True)
            acc = acc + _dot(vt_blocks[j][h * HEAD_DIM:(h + 1) * HEAD_DIM, :], p.astype(BF16))
        outs.append(acc / den)
    o_ref[...] = jnp.concatenate(outs, axis=0).T


def _moba(y3):
    B, L, _ = y3.shape
    BLK = MOBA_BLOCK
    nb = L // BLK
    outs = []
    for qb in range(nb):
        S = (qb + 1) * BLK
        outs.append(pl.pallas_call(
            functools.partial(_moba_kernel, qb=qb),
            grid=(B,),
            in_specs=[pl.BlockSpec((None, BLK, 256), lambda b, qb=qb: (b, qb, _DST["m_q"][0] // 256)),
                      pl.BlockSpec((None, S, 256), lambda b: (b, 0, _DST["m_k"][0] // 256)),
                      pl.BlockSpec((None, S, 256), lambda b: (b, 0, _DST["m_v"][0] // 256))],
            out_specs=pl.BlockSpec((None, BLK, 256), lambda b: (b, 0, 0)),
            out_shape=jax.ShapeDtypeStruct((B, BLK, 256), F32),
            compiler_params=_cparams(("parallel",)),
            name=f"moba_{qb}",
        )(y3, y3, y3))
    return jnp.stack(outs, axis=1).reshape(B * L, MOBA_HEADS * HEAD_DIM)


INT_MIN = -2 ** 31
NEG_INF_KEY = -2139095041
DSA_ROW_BLOCK = 128


def _dsa_kernel(q_ref, k_ref, v_ref, iq_ref, ik_ref, misc_ref, o_ref, key_ref, sel_ref, *, qt, top):
    TQ = q_ref.shape[0]
    S = k_ref.shape[0]
    iq_t = iq_ref[...].T.astype(BF16)
    w_t = misc_ref[...].T[MISC_W0:MISC_W0 + IDX_HEADS, :] * IDX_HEADS ** -0.5
    q_pairs = [iq_t[p * LANES:(p + 1) * LANES, :] for p in range(IDX_HEADS // 2)]
    w_rows = [w_t[h:h + 1, :] for h in range(IDX_HEADS)]
    for j in range(S // DSA_ROW_BLOCK):
        rows = slice(j * DSA_ROW_BLOCK, (j + 1) * DSA_ROW_BLOCK)
        ik = ik_ref[rows, :]
        ik_lo = ik.astype(BF16)
        ik_hi = pltpu.roll(ik, IDX_DIM, 1).astype(BF16)
        score = None
        for p, q_pair in enumerate(q_pairs):
            for half, ik_half in enumerate((ik_lo, ik_hi)):
                term = jnp.maximum(_dot(ik_half, q_pair), 0.0) * w_rows[2 * p + half]
                score = term if score is None else score + term
        score = jnp.where(score == 0.0, 0.0, score)
        if (j + 1) * DSA_ROW_BLOCK > qt * TQ + 1:
            krow = lax.broadcasted_iota(jnp.int32, score.shape, 0) + j * DSA_ROW_BLOCK
            qpos = lax.broadcasted_iota(jnp.int32, score.shape, 1) + qt * TQ
            score = jnp.where(krow <= qpos, score, -jnp.inf)
        bits = pltpu.bitcast(score, jnp.int32)
        key_ref[rows, :] = bits ^ ((bits >> 31) & 0x7FFFFFFF)

    def count_ge(t):
        return jnp.sum(jnp.where(key_ref[...] >= t, 1.0, 0.0), axis=0, keepdims=True)

    kf = float(top)
    n_nonneg = count_ge(jnp.zeros((1, TQ), jnp.int32))
    start = n_nonneg >= kf
    prefix = jnp.where(start, 0, INT_MIN).astype(jnp.int32)
    n_ge = jnp.where(start, n_nonneg, float(S))

    def bit_step(i, carry):
        prefix, n_ge = carry
        cand = prefix | jnp.left_shift(jnp.int32(1), 30 - i)
        n_cand = count_ge(cand)
        take = n_cand >= kf
        return jnp.where(take, cand, prefix), jnp.where(take, n_cand, n_ge)

    thr, n_ge = lax.fori_loop(0, 31, bit_step, (prefix, n_ge))
    real = thr > NEG_INF_KEY
    thr_sel = jnp.maximum(thr, NEG_INF_KEY + 1)
    sel_ref[...] = jnp.where(key_ref[...] >= thr_sel, 0.0, -jnp.inf)

    @pl.when(jnp.max(jnp.where(real, n_ge, 0.0)) > kf)
    def _():
        n_gt = jnp.sum(jnp.where(key_ref[...] > thr, 1.0, 0.0), axis=0, keepdims=True)
        r = lax.broadcasted_iota(jnp.int32, (TQ, TQ), 0)
        c = lax.broadcasted_iota(jnp.int32, (TQ, TQ), 1)
        before = jnp.where(c < r, 1.0, 0.0).astype(BF16)
        room = kf - n_gt
        seen = jnp.zeros_like(n_gt)
        for i in range(S // TQ):
            rows = slice(i * TQ, (i + 1) * TQ)
            k_i = key_ref[rows, :]
            eq_i = k_i == thr
            eq_f = jnp.where(eq_i, 1.0, 0.0)
            n_eq_before = seen + _dot(before, eq_f.astype(BF16))
            keep = (k_i > thr) | (eq_i & (n_eq_before < room))
            valid = (r + i * TQ) <= (c + qt * TQ)
            sel_ref[rows, :] = jnp.where(keep & valid, 0.0, -jnp.inf)
            seen = seen + jnp.sum(eq_f, axis=0, keepdims=True)

    q_t = q_ref[...].T
    feat_head = lax.broadcasted_iota(jnp.int32, q_t.shape, 0) // HEAD_DIM
    kb = k_ref[...].astype(BF16)
    v_t = v_ref[...].T.astype(BF16)
    outs = []
    for h in range(DSA_HEADS):
        qth = jnp.where(feat_head == h, q_t, 0.0).astype(BF16)
        s = _dot(kb, qth) + sel_ref[...]
        m = jnp.max(s, axis=0, keepdims=True)
        p = jnp.exp(s - m)
        den = jnp.sum(p, axis=0, keepdims=True)
        outs.append(_dot(v_t[h * HEAD_DIM:(h + 1) * HEAD_DIM, :], p.astype(BF16)) / den)
    o_ref[...] = jnp.concatenate(outs, axis=0).T


def _dsa(y3, tq=256):
    B, L, _ = y3.shape
    top = min(DSA_TOPK, L // 4)
    outs = []
    for qt in range(L // tq):
        S = (qt + 1) * tq
        outs.append(pl.pallas_call(
            functools.partial(_dsa_kernel, qt=qt, top=top),
            grid=(B,),
            in_specs=[pl.BlockSpec((None, tq, 256), lambda b, qt=qt: (b, qt, _DST["s_q"][0] // 256)),
                      pl.BlockSpec((None, S, 256), lambda b: (b, 0, _DST["s_k"][0] // 256)),
                      pl.BlockSpec((None, S, 256), lambda b: (b, 0, _DST["s_v"][0] // 256)),
                      pl.BlockSpec((None, tq, 512), lambda b, qt=qt: (b, qt, _DST["i_q"][0] // 512)),
                      pl.BlockSpec((None, S, LANES), lambda b: (b, 0, _DST["i_k"][0] // LANES)),
                      pl.BlockSpec((None, tq, LANES), lambda b, qt=qt: (b, qt, _DST["misc"][0] // LANES))],
            out_specs=pl.BlockSpec((None, tq, 256), lambda b: (b, 0, 0)),
            out_shape=jax.ShapeDtypeStruct((B, tq, 256), F32),
            scratch_shapes=[pltpu.VMEM((S, tq), jnp.int32), pltpu.VMEM((S, tq), F32)],
            compiler_params=_cparams(("parallel",)),
            name=f"dsa_{qt}",
        )(y3, y3, y3, y3, y3, y3))
    return jnp.stack(outs, axis=1).reshape(B * L, DSA_HEADS * HEAD_DIM)


def _out_proj_kernel(h_ref, a_ref, b_ref, c_ref, w_ref, g_ref, beta_ref, rwh_ref, rwl_ref, rb_ref,
                     h1_ref, cls_ref, rank_ref, cnt_ref, carry_ref, *, alpha):
    @pl.when(pl.program_id(0) == 0)
    def _():
        carry_ref[...] = jnp.zeros_like(carry_ref)

    D = h_ref.shape[1]
    tm = h_ref.shape[0]
    wa, wb = a_ref.shape[1], b_ref.shape[1]
    mix = _dot(a_ref[...].astype(BF16), w_ref[0:wa, :])
    mix = mix + _dot(b_ref[...].astype(BF16), w_ref[wa:wa + wb, :])
    mix = mix + _dot(c_ref[...].astype(BF16), w_ref[wa + wb:, :])
    h1 = _layer_norm(alpha * h_ref[...] + mix, g_ref[...], beta_ref[...])
    h1_ref[:, 0:D] = h1
    h1_hi = h1.astype(BF16)
    h1_lo = (h1 - h1_hi.astype(F32)).astype(BF16)
    logits = _dot(h1_hi, rwh_ref[...]) + _dot(h1_lo, rwh_ref[...]) + _dot(h1_hi, rwl_ref[...])
    logits_t = logits.T[0:N_EXPERTS, :]
    scores = 1.0 / (1.0 + jnp.exp(-logits_t))
    sel = scores + rb_ref[...]
    rows = [sel[e:e + 1, :] for e in range(N_EXPERTS)]
    best_score, best = None, None
    for g in range(N_GROUPS):
        r = rows[g * EXPERTS_PER_GROUP:(g + 1) * EXPERTS_PER_GROUP]
        gs = None
        for i in range(EXPERTS_PER_GROUP):
            for j in range(i + 1, EXPERTS_PER_GROUP):
                pair = r[i] + r[j]
                gs = pair if gs is None else jnp.maximum(gs, pair)
        if g == 0:
            best_score, best = gs, jnp.zeros(gs.shape, jnp.int32)
        else:
            better = gs > best_score
            best = jnp.where(better, g, best)
            best_score = jnp.where(better, gs, best_score)
    expert = lax.broadcasted_iota(jnp.int32, sel.shape, 0)
    in_group = (expert // EXPERTS_PER_GROUP) == best
    masked = jnp.where(in_group, sel, -jnp.inf)
    rank = jnp.zeros(sel.shape, F32)
    for e in range(N_EXPERTS):
        other = masked[e:e + 1, :]
        ahead = (other > masked) | ((other == masked) & (e < expert))
        rank = rank + jnp.where(ahead, 1.0, 0.0)
    picked = jnp.where((rank < TOP_K) & in_group, 1.0, 0.0)
    chosen = picked * scores
    comb_t = chosen / jnp.sum(chosen, axis=0, keepdims=True)
    G = EXPERTS_PER_GROUP
    m = [sum(picked[g * G + i:g * G + i + 1, :] for g in range(N_GROUPS)) for i in range(G)]
    cw = [sum(comb_t[g * G + i:g * G + i + 1, :] for g in range(N_GROUPS)) for i in range(G)]
    pair = jnp.zeros_like(m[0])
    for idx, (i, j) in enumerate(_PAIRS):
        pair = pair + float(idx) * m[i] * m[j]
    cls = best * len(_PAIRS) + pair.astype(jnp.int32)
    w_lo = jnp.where(m[0] > 0, cw[0], jnp.where(m[1] > 0, cw[1], cw[2]))
    w_hi = jnp.where(m[3] > 0, cw[3], jnp.where(m[2] > 0, cw[2], cw[1]))
    pad = jnp.zeros((LANES - 2, tm), F32)
    h1_ref[:, D:D + LANES] = jnp.concatenate([w_lo, w_hi, pad], axis=0).T
    onehot = jnp.where(lax.broadcasted_iota(jnp.int32, (CLASS_ROWS, tm), 0) == cls, 1.0, 0.0)
    r = lax.broadcasted_iota(jnp.int32, (tm, tm), 0)
    c = lax.broadcasted_iota(jnp.int32, (tm, tm), 1)
    earlier = jnp.where(r < c, 1.0, 0.0).astype(BF16)
    before = _dot(onehot.astype(BF16), earlier) + carry_ref[...]
    cls_ref[...] = cls
    rank_ref[...] = jnp.sum(onehot * before, axis=0, keepdims=True).astype(jnp.int32)
    carry = carry_ref[...] + jnp.sum(onehot, axis=1, keepdims=True)
    carry_ref[...] = carry
    cnt_ref[...] = carry[:, 0:LANES]


def _out_proj(h, out_a, out_b, out_c, w_out, ln_g, ln_b, router_w, router_bias, alpha, tm=512):
    T, D = h.shape
    nt = T // tm
    rw = jnp.zeros((D, LANES), F32).at[:, :N_EXPERTS].set(router_w)
    rw_hi = rw.astype(BF16)
    rw_lo = (rw - rw_hi.astype(F32)).astype(BF16)
    rb = jnp.broadcast_to(router_bias.astype(F32).reshape(N_EXPERTS, 1), (N_EXPERTS, tm))
    row = lambda i: (i, 0)
    full = lambda i: (0, 0)
    vec = pl.BlockSpec((None, 1, tm), lambda i: (i, 0, 0))
    return pl.pallas_call(
        functools.partial(_out_proj_kernel, alpha=alpha),
        grid=(nt,),
        in_specs=[pl.BlockSpec((tm, D), row),
                  pl.BlockSpec((tm, out_a.shape[1]), row),
                  pl.BlockSpec((tm, out_b.shape[1]), row),
                  pl.BlockSpec((tm, out_c.shape[1]), row),
                  pl.BlockSpec(w_out.shape, full),
                  pl.BlockSpec((1, D), full), pl.BlockSpec((1, D), full),
                  pl.BlockSpec((D, LANES), full), pl.BlockSpec((D, LANES), full),
                  pl.BlockSpec((N_EXPERTS, tm), full)],
        out_specs=[pl.BlockSpec((tm, D + LANES), row), vec, vec, pl.BlockSpec((CLASS_ROWS, LANES), full)],
        out_shape=[jax.ShapeDtypeStruct((T, D + LANES), F32),
                   jax.ShapeDtypeStruct((nt, 1, tm), jnp.int32),
                   jax.ShapeDtypeStruct((nt, 1, tm), jnp.int32),
                   jax.ShapeDtypeStruct((CLASS_ROWS, LANES), F32)],
        scratch_shapes=[pltpu.VMEM((CLASS_ROWS, tm), F32)],
        compiler_params=_cparams(("arbitrary",)),
        name="out_proj",
    )(h, out_a, out_b, out_c, w_out.astype(BF16), ln_g.reshape(1, D), ln_b.reshape(1, D), rw_hi, rw_lo, rb)


def _plan(cls, rank, counts, tile):
    n_tiles = cls.size // tile + N_CLASSES
    cnt = counts[:N_CLASSES, 0].astype(jnp.int32)
    tiles_per_class = (cnt + tile - 1) // tile
    tile_end = jnp.cumsum(tiles_per_class)
    row_start = (tile_end - tiles_per_class) * tile
    pos = (row_start[cls.reshape(-1)] + rank.reshape(-1)).astype(jnp.int32)
    t = jnp.arange(n_tiles, dtype=jnp.int32)
    tile_cls = jnp.minimum(jnp.sum((t[:, None] >= tile_end[None, :]).astype(jnp.int32), axis=1), N_CLASSES - 1)
    used = (t < tile_end[-1]).astype(jnp.int32)
    lo = jnp.array([p[0] for p in _PAIRS], jnp.int32)
    hi = jnp.array([p[1] for p in _PAIRS], jnp.int32)
    group = tile_cls // len(_PAIRS)
    e_lo = group * EXPERTS_PER_GROUP + lo[tile_cls % len(_PAIRS)]
    e_hi = group * EXPERTS_PER_GROUP + hi[tile_cls % len(_PAIRS)]
    return pos, e_lo, e_hi, used, n_tiles


ROW_DMA_UNROLL = 8


def _row_copies(pos_hbm, pos_smem, idx_sem, n_rows, make_copy):
    base = pl.multiple_of(pl.program_id(0) * n_rows, n_rows)
    idx_copy = pltpu.make_async_copy(pos_hbm.at[pl.ds(base, n_rows)], pos_smem, idx_sem)
    idx_copy.start()
    idx_copy.wait()

    def issue(g, carry):
        for u in range(ROW_DMA_UNROLL):
            r = g * ROW_DMA_UNROLL + u
            make_copy(r, pos_smem[r]).start(priority=u % 2)
        return carry

    lax.fori_loop(0, n_rows // ROW_DMA_UNROLL, issue, 0)

    def drain(r, carry):
        make_copy(r, pos_smem[r]).wait()
        return carry

    lax.fori_loop(0, n_rows, drain, 0, unroll=8)


def _dispatch_kernel(pos_hbm, x_ref, zero_hbm, xs_hbm, pos_smem, idx_sem, sem):
    del zero_hbm
    n_rows = x_ref.shape[0]
    _row_copies(pos_hbm, pos_smem, idx_sem, n_rows,
                lambda r, p: pltpu.make_async_copy(x_ref.at[r], xs_hbm.at[p], sem))


def _dispatch(x_aug, pos, n_rows_sorted, tm=2048):
    T, W = x_aug.shape
    return pl.pallas_call(
        _dispatch_kernel,
        grid=(T // tm,),
        in_specs=[pl.BlockSpec(memory_space=pl.ANY),
                  pl.BlockSpec((tm, W), lambda i: (i, 0)),
                  pl.BlockSpec(memory_space=pl.ANY)],
        out_specs=pl.BlockSpec(memory_space=pl.ANY),
        out_shape=jax.ShapeDtypeStruct((n_rows_sorted, W), F32),
        scratch_shapes=[pltpu.SMEM((tm,), jnp.int32), pltpu.SemaphoreType.DMA(()), pltpu.SemaphoreType.DMA(())],
        input_output_aliases={2: 0},
        compiler_params=_cparams(("arbitrary",)),
        name="moe_dispatch",
    )(pos, x_aug, jnp.zeros((n_rows_sorted, W), F32))


def _combine_kernel(pos_hbm, ys_hbm, o_ref, pos_smem, idx_sem, sem):
    n_rows = o_ref.shape[0]
    _row_copies(pos_hbm, pos_smem, idx_sem, n_rows,
                lambda r, p: pltpu.make_async_copy(ys_hbm.at[p], o_ref.at[r], sem))


def _combine(ys, pos, tm=2048):
    T = pos.shape[0]
    D = ys.shape[1]
    return pl.pallas_call(
        _combine_kernel,
        grid=(T // tm,),
        in_specs=[pl.BlockSpec(memory_space=pl.ANY), pl.BlockSpec(memory_space=pl.ANY)],
        out_specs=pl.BlockSpec((tm, D), lambda i: (i, 0)),
        out_shape=jax.ShapeDtypeStruct((T, D), F32),
        scratch_shapes=[pltpu.SMEM((tm,), jnp.int32), pltpu.SemaphoreType.DMA(()), pltpu.SemaphoreType.DMA(())],
        compiler_params=_cparams(("arbitrary",)),
        name="moe_combine",
    )(pos, ys)


def _moe_kernel(e_lo_ref, e_hi_ref, used_ref, xs_ref, wg1_ref, wu1_ref, wd1_ref, wg2_ref, wu2_ref, wd2_ref,
                g_ref, beta_ref, ys_ref, *, alpha):
    del e_lo_ref, e_hi_ref
    D = ys_ref.shape[1]
    used = used_ref[pl.program_id(0)] > 0

    @pl.when(used)
    def _():
        x = xs_ref[:, 0:D]
        w = xs_ref[:, D:D + LANES]
        xb = x.astype(BF16)

        def expert(wg_ref, wu_ref, wd_ref):
            act = _silu(_dot(xb, wg_ref[...])) * _dot(xb, wu_ref[...])
            return _dot(act.astype(BF16), wd_ref[...])

        ffn = w[:, 0:1] * expert(wg1_ref, wu1_ref, wd1_ref) + w[:, 1:2] * expert(wg2_ref, wu2_ref, wd2_ref)
        ys_ref[...] = _layer_norm(alpha * x + ffn, g_ref[...], beta_ref[...])

    @pl.when(jnp.logical_not(used))
    def _():
        ys_ref[...] = jnp.zeros_like(ys_ref)


def _moe(xs, e_lo, e_hi, used, w_gate, w_up, w_down, ln_g, ln_b, alpha, tile):
    R, W = xs.shape
    E, D, F = w_gate.shape
    first = lambda i, lo, hi, u: (lo[i], 0, 0)
    second = lambda i, lo, hi, u: (hi[i], 0, 0)
    const = lambda i, lo, hi, u: (0, 0)
    wg, wu, wd = w_gate.astype(BF16), w_up.astype(BF16), w_down.astype(BF16)
    grid_spec = pltpu.PrefetchScalarGridSpec(
        num_scalar_prefetch=3,
        grid=(R // tile,),
        in_specs=[pl.BlockSpec((tile, W), lambda i, lo, hi, u: (i, 0)),
                  pl.BlockSpec((None, D, F), first), pl.BlockSpec((None, D, F), first),
                  pl.BlockSpec((None, F, D), first),
                  pl.BlockSpec((None, D, F), second), pl.BlockSpec((None, D, F), second),
                  pl.BlockSpec((None, F, D), second),
                  pl.BlockSpec((1, D), const), pl.BlockSpec((1, D), const)],
        out_specs=pl.BlockSpec((tile, D), lambda i, lo, hi, u: (i, 0)),
    )
    return pl.pallas_call(
        functools.partial(_moe_kernel, alpha=alpha),
        grid_spec=grid_spec,
        out_shape=jax.ShapeDtypeStruct((R, D), F32),
        compiler_params=_cparams(("arbitrary",)),
        name="moe",
    )(e_lo, e_hi, used, xs, wg, wu, wd, wg, wu, wd, ln_g.reshape(1, D), ln_b.reshape(1, D))


def _moe_block(h1_aug, cls, rank, counts, w_gate, w_up, w_down, ln_g, ln_b, alpha, tile=MOE_TILE):
    pos, e_lo, e_hi, used, n_tiles = _plan(cls, rank, counts, tile)
    xs = _dispatch(h1_aug, pos, n_tiles * tile)
    ys = _moe(xs, e_lo, e_hi, used, w_gate, w_up, w_down, ln_g, ln_b, alpha, tile)
    return _combine(ys, pos)


def kernel(x, positions, w_in, gla_gate_up, gla_gate_bias, gla_norm_gain, w_out, ln_mix_g, ln_mix_b, router_w, router_bias, w_expert_gate, w_expert_up, w_expert_down, ln_ffn_g, ln_ffn_b):
    B, L, D = x.shape
    depth = w_in.shape[0]
    alpha = (2 * depth) ** 0.25
    T = B * L
    tabs = _rope_tables(positions)
    h = x.reshape(T, D)
    for layer in range(depth):
        y = _in_proj(h, _pack_w_in(w_in[layer]), tabs)
        out_a = _gla(y, B, L, gla_gate_up[layer], gla_gate_bias[layer], gla_norm_gain[layer])
        y3 = y.reshape(B, L, NP)
        out_b = _moba(y3)
        out_c = _dsa(y3)
        h1_aug, cls, rank, counts = _out_proj(h, out_a, out_b, out_c, w_out[layer], ln_mix_g[layer],
                                              ln_mix_b[layer], router_w, router_bias, alpha)
        h = _moe_block(h1_aug, cls, rank, counts, w_expert_gate[layer], w_expert_up[layer],
                       w_expert_down[layer], ln_ffn_g[layer], ln_ffn_b[layer], alpha)
    return h.reshape(B, L, D)
```

```python
import functools

import jax
import jax.numpy as jnp
from jax import lax
from jax.experimental import pallas as pl
from jax.experimental.pallas import tpu as pltpu

F32 = jnp.float32
BF16 = jnp.bfloat16

HEAD_DIM = 64
GLA_HEADS, GLA_DK, GLA_DV = 4, 64, 128
GLA_GATE_RANK = 16
GLA_GATE_NORM = 16.0
GLA_CHUNK = 64
MOBA_HEADS, MOBA_BLOCK, MOBA_TOPK = 4, 256, 3
DSA_HEADS, DSA_TOPK = 4, 256
IDX_HEADS, IDX_DIM = 8, 64
ROPE_THETA = 500000.0
ROPE_HALF = 8
N_EXPERTS, N_GROUPS, TOP_K = 16, 4, 2
EXPERTS_PER_GROUP = N_EXPERTS // N_GROUPS
_PAIRS = tuple((i, j) for i in range(EXPERTS_PER_GROUP) for j in range(i + 1, EXPERTS_PER_GROUP))
N_CLASSES = N_GROUPS * len(_PAIRS)
CLASS_ROWS = 32
MOE_TILE = 512
LN_EPS = 1e-5
RMS_EPS = 1e-6

LANES = 128
VMEM_LIMIT = 56 * 1024 * 1024

_SRC = dict(g_q=(0, 256), g_k=(256, 256), g_v=(512, 512), g_rank=(1024, 16), g_gate=(1040, 512),
            m_q=(1552, 256), m_k=(1808, 256), m_v=(2064, 256),
            s_q=(2320, 256), s_k=(2576, 256), s_v=(2832, 256),
            i_q=(3088, 512), i_k=(3600, 64), i_w=(3664, 8))
_DST = dict(g_q=(0, 256), g_k=(256, 256), g_v=(512, 512), g_gate=(1024, 512),
            m_q=(1536, 256), m_k=(1792, 256), m_v=(2048, 256),
            s_q=(2304, 256), s_k=(2560, 256), s_v=(2816, 256),
            i_q=(3072, 512), i_k=(3584, 128), misc=(3712, 128))
NP = 3840
MISC_RANK0 = 0
MISC_W0 = 16
_ROPED = ("m_q", "m_k", "s_q", "s_k", "i_q", "i_k")
_SCALED = dict(g_q=GLA_DK ** -0.5, m_q=HEAD_DIM ** -0.5, s_q=HEAD_DIM ** -0.5, i_q=IDX_DIM ** -0.5)


def _cparams(sem):
    return pltpu.CompilerParams(dimension_semantics=sem, vmem_limit_bytes=VMEM_LIMIT)


def _dot(a, b):
    return jnp.dot(a, b, preferred_element_type=F32)


def _dot_split(a, b):
    a_hi, a_lo = _split_bf16(a, 2)
    b_hi, b_lo = _split_bf16(b, 2)
    return _dot(a_hi, b_hi) + _dot(a_lo, b_hi) + _dot(a_hi, b_lo)


def _split_bf16(x, terms):
    parts = []
    for _ in range(terms - 1):
        p = x.astype(BF16)
        parts.append(p)
        x = x - p.astype(F32)
    parts.append(x.astype(BF16))
    return parts


def _dot_nt(a, b):
    return lax.dot_general(a, b, (((1,), (1,)), ((), ())), preferred_element_type=F32)


def _dot_tn(a, b):
    return lax.dot_general(a, b, (((0,), (0,)), ((), ())), preferred_element_type=F32)


def _layer_norm(x, g, b):
    mu = jnp.mean(x, -1, keepdims=True)
    xc = x - mu
    var = jnp.mean(xc * xc, -1, keepdims=True)
    return xc * lax.rsqrt(var + LN_EPS) * g + b


def _silu(x):
    return x * (1.0 / (1.0 + jnp.exp(-x)))


def _rope_table_kernel(pos_ref, inv_ref, c_ref, s1_ref, s2_ref):
    ang = pos_ref[...].astype(F32) * inv_ref[...]
    lane = lax.broadcasted_iota(jnp.int32, ang.shape, 1) % HEAD_DIM
    cos, sin = jnp.cos(ang), jnp.sin(ang)
    first = lane < ROPE_HALF
    second = (lane >= ROPE_HALF) & (lane < 2 * ROPE_HALF)
    c_ref[...] = jnp.where(first | second, cos, 1.0)
    s1_ref[...] = jnp.where(first, -sin, 0.0)
    s2_ref[...] = jnp.where(second, sin, 0.0)


def _rope_tables(positions, tm=512):
    T = positions.size
    pos = positions.reshape(T, 1)
    j = (jnp.arange(LANES) % ROPE_HALF).astype(F32)
    inv = (ROPE_THETA ** (-j / ROPE_HALF)).reshape(1, LANES)
    out = jax.ShapeDtypeStruct((T, LANES), F32)
    return pl.pallas_call(
        _rope_table_kernel,
        grid=(T // tm,),
        in_specs=[pl.BlockSpec((tm, 1), lambda i: (i, 0)), pl.BlockSpec((1, LANES), lambda i: (0, 0))],
        out_specs=[pl.BlockSpec((tm, LANES), lambda i: (i, 0))] * 3,
        out_shape=[out, out, out],
        compiler_params=_cparams(("parallel",)),
        name="rope_tables",
    )(pos, inv)


def _in_proj_kernel(h_ref, w_ref, c_ref, s1_ref, s2_ref, y_ref):
    hb = h_ref[...].astype(BF16)
    c, s1, s2 = c_ref[...], s1_ref[...], s2_ref[...]
    for name, (start, width) in _DST.items():
        y = _dot(hb, w_ref[:, start:start + width])
        if name in _SCALED:
            y = y * _SCALED[name]
        if name in _ROPED:
            blocks = []
            for o in range(0, width, LANES):
                yb = y[:, o:o + LANES]
                up = pltpu.roll(yb, LANES - ROPE_HALF, 1)
                down = pltpu.roll(yb, ROPE_HALF, 1)
                blocks.append(yb * c + up * s1 + down * s2)
            y = blocks[0] if len(blocks) == 1 else jnp.concatenate(blocks, axis=1)
        y_ref[:, start:start + width] = y


def _pack_w_in(w_in):
    D = w_in.shape[0]
    cols = []
    for name, (start, width) in _DST.items():
        if name == "misc":
            r0, rw = _SRC["g_rank"]
            w0, ww = _SRC["i_w"]
            blk = jnp.zeros((D, width), w_in.dtype)
            blk = blk.at[:, MISC_RANK0:MISC_RANK0 + rw].set(w_in[:, r0:r0 + rw])
            blk = blk.at[:, MISC_W0:MISC_W0 + ww].set(w_in[:, w0:w0 + ww])
        else:
            s0, sw = _SRC[name]
            blk = w_in[:, s0:s0 + sw]
            if sw < width:
                blk = jnp.pad(blk, ((0, 0), (0, width - sw)))
        cols.append(blk)
    return jnp.concatenate(cols, axis=1).astype(BF16)


def _in_proj(h, w_packed, tabs, tm=512):
    T, D = h.shape
    tab_spec = pl.BlockSpec((tm, LANES), lambda i: (i, 0))
    return pl.pallas_call(
        _in_proj_kernel,
        grid=(T // tm,),
        in_specs=[pl.BlockSpec((tm, D), lambda i: (i, 0)),
                  pl.BlockSpec((D, NP), lambda i: (0, 0)),
                  tab_spec, tab_spec, tab_spec],
        out_specs=pl.BlockSpec((tm, NP), lambda i: (i, 0)),
        out_shape=jax.ShapeDtypeStruct((T, NP), F32),
        compiler_params=_cparams(("parallel",)),
        name="in_proj",
    )(h, w_packed, *tabs)


def _gla_kernel(q_ref, k_ref, v_ref, g_ref, misc_ref, up_ref, bias_ref, gain_ref, o_ref, st_ref, *, tl):
    @pl.when(pl.program_id(1) == 0)
    def _():
        st_ref[...] = jnp.zeros_like(st_ref)

    C = GLA_CHUNK
    dkw, dvw = GLA_HEADS * GLA_DK, GLA_HEADS * GLA_DV
    z = _dot_split(misc_ref[...], up_ref[...]) + bias_ref[...]
    log_a = (jnp.minimum(z, 0.0) - jnp.log(1.0 + jnp.exp(-jnp.abs(z)))) / GLA_GATE_NORM
    log_a_terms = _split_bf16(log_a, 3)
    ri = lax.broadcasted_iota(jnp.int32, (C, C), 0)
    ci = lax.broadcasted_iota(jnp.int32, (C, C), 1)
    causal = ci <= ri
    tri = jnp.where(causal, 1.0, 0.0).astype(BF16)
    head_rows = lax.broadcasted_iota(jnp.int32, (GLA_HEADS * C, dkw), 0) // C
    head_lanes = lax.broadcasted_iota(jnp.int32, (GLA_HEADS * C, dkw), 1) // GLA_DK
    own_head = head_rows == head_lanes
    sr = lax.broadcasted_iota(jnp.int32, (dvw, dkw), 0) // GLA_DV
    sc = lax.broadcasted_iota(jnp.int32, (dvw, dkw), 1) // GLA_DK
    diag = sr == sc
    gain = gain_ref[...]

    for c in range(tl // C):
        rows = slice(c * C, (c + 1) * C)
        b = sum(_dot(tri, term[rows]) for term in log_a_terms)
        b_last = b[C - 1:C, :]
        q_dec = q_ref[rows, :] * jnp.exp(b)
        k_c = k_ref[rows, :]
        k_inv = (k_c * jnp.exp(-b)).astype(BF16)
        k_dec = (k_c * jnp.exp(b_last - b)).astype(BF16)
        v_c = v_ref[rows, :].astype(BF16)
        q_dec_b = q_dec.astype(BF16)
        st = st_ref[...]
        o = _dot_nt(q_dec_b, st.astype(BF16))
        q_heads = jnp.where(own_head, jnp.concatenate([q_dec] * GLA_HEADS, axis=0), 0.0).astype(BF16)
        a_all = _dot_nt(q_heads, k_inv)
        intra = []
        for h in range(GLA_HEADS):
            a = jnp.where(causal, a_all[h * C:(h + 1) * C, :], 0.0)
            intra.append(_dot(a.astype(BF16), v_c[:, h * GLA_DV:(h + 1) * GLA_DV]))
        o = o + jnp.concatenate(intra, axis=1)
        kv_t = _dot_tn(v_c, k_dec)
        st_ref[...] = jnp.exp(b_last) * st + jnp.where(diag, kv_t, 0.0)
        outs = []
        for h in range(GLA_HEADS):
            oh = o[:, h * GLA_DV:(h + 1) * GLA_DV]
            ms = jnp.mean(oh * oh, -1, keepdims=True)
            outs.append(oh * lax.rsqrt(ms + RMS_EPS) * gain)
        o_ref[rows, :] = jnp.concatenate(outs, axis=1) * _silu(g_ref[rows, :])


def _gla(y, B, L, gate_up, gate_bias, norm_gain, tl=512):
    T = B * L
    nt = L // tl
    up = jnp.zeros((LANES, GLA_HEADS * GLA_DK), F32).at[MISC_RANK0:MISC_RANK0 + GLA_GATE_RANK].set(gate_up)
    row = lambda b, t: b * nt + t
    return pl.pallas_call(
        functools.partial(_gla_kernel, tl=tl),
        grid=(B, nt),
        in_specs=[pl.BlockSpec((tl, 256), lambda b, t: (row(b, t), _DST["g_q"][0] // 256)),
                  pl.BlockSpec((tl, 256), lambda b, t: (row(b, t), _DST["g_k"][0] // 256)),
                  pl.BlockSpec((tl, 512), lambda b, t: (row(b, t), _DST["g_v"][0] // 512)),
                  pl.BlockSpec((tl, 512), lambda b, t: (row(b, t), _DST["g_gate"][0] // 512)),
                  pl.BlockSpec((tl, LANES), lambda b, t: (row(b, t), _DST["misc"][0] // LANES)),
                  pl.BlockSpec((LANES, 256), lambda b, t: (0, 0)),
                  pl.BlockSpec((1, 256), lambda b, t: (0, 0)),
                  pl.BlockSpec((1, GLA_DV), lambda b, t: (0, 0))],
        out_specs=pl.BlockSpec((tl, 512), lambda b, t: (row(b, t), 0)),
        out_shape=jax.ShapeDtypeStruct((T, GLA_HEADS * GLA_DV), F32),
        scratch_shapes=[pltpu.VMEM((GLA_HEADS * GLA_DV, GLA_HEADS * GLA_DK), F32)],
        compiler_params=_cparams(("parallel", "arbitrary")),
        name="gla",
    )(y, y, y, y, y, up, gate_bias.reshape(1, -1), norm_gain.reshape(1, -1))


def _moba_kernel(q_ref, k_ref, v_ref, o_ref, *, qb):
    BLK = MOBA_BLOCK
    qt = q_ref[...].T
    feat_head = lax.broadcasted_iota(jnp.int32, qt.shape, 0) // HEAD_DIM
    krow = lax.broadcasted_iota(jnp.int32, (BLK, BLK), 0)
    qcol = lax.broadcasted_iota(jnp.int32, (BLK, BLK), 1)
    own_bias = jnp.where(krow <= qcol, 0.0, -jnp.inf)
    k_blocks = [k_ref[j * BLK:(j + 1) * BLK, :] for j in range(qb + 1)]
    kb_blocks = [kb.astype(BF16) for kb in k_blocks]
    vt_blocks = [v_ref[j * BLK:(j + 1) * BLK, :].T.astype(BF16) for j in range(qb + 1)]
    if qb > 0:
        k_mean = jnp.concatenate([jnp.mean(kb, axis=0, keepdims=True) for kb in k_blocks[:qb]], axis=0)
    outs = []
    for h in range(MOBA_HEADS):
        qth = jnp.where(feat_head == h, qt, 0.0)
        qth_b = qth.astype(BF16)
        biases = []
        if qb > 0:
            gate = _dot_split(k_mean, qth)
            g = [gate[j:j + 1, :] for j in range(qb)]
            for j in range(qb):
                rank = jnp.zeros_like(g[j])
                for i in range(qb):
                    if i == j:
                        continue
                    ahead = (g[i] >= g[j]) if i < j else (g[i] > g[j])
                    rank = rank + jnp.where(ahead, 1.0, 0.0)
                biases.append(jnp.where(rank < MOBA_TOPK, 0.0, -jnp.inf))
        s = [_dot(kb_blocks[j], qth_b) + biases[j] for j in range(qb)]
        s.append(_dot(kb_blocks[qb], qth_b) + own_bias)
        m = jnp.max(s[0], axis=0, keepdims=True)
        for sj in s[1:]:
            m = jnp.maximum(m, jnp.max(sj, axis=0, keepdims=True))
        den = jnp.zeros_like(m)
        acc = jnp.zeros((HEAD_DIM, BLK), F32)
        for j in range(qb + 1):
            p = jnp.exp(s[j] - m)
            den = den + jnp.sum(p, axis=0, keepdims=True)
            acc = acc + _dot(vt_blocks[j][h * HEAD_DIM:(h + 1) * HEAD_DIM, :], p.astype(BF16))
        outs.append(acc / den)
    o_ref[...] = jnp.concatenate(outs, axis=0).T


def _moba(y3):
    B, L, _ = y3.shape
    BLK = MOBA_BLOCK
    nb = L // BLK
    outs = []
    for qb in range(nb):
        S = (qb + 1) * BLK
        outs.append(pl.pallas_call(
            functools.partial(_moba_kernel, qb=qb),
            grid=(B,),
            in_specs=[pl.BlockSpec((None, BLK, 256), lambda b, qb=qb: (b, qb, _DST["m_q"][0] // 256)),
                      pl.BlockSpec((None, S, 256), lambda b: (b, 0, _DST["m_k"][0] // 256)),
                      pl.BlockSpec((None, S, 256), lambda b: (b, 0, _DST["m_v"][0] // 256))],
            out_specs=pl.BlockSpec((None, BLK, 256), lambda b: (b, 0, 0)),
            out_shape=jax.ShapeDtypeStruct((B, BLK, 256), F32),
            compiler_params=_cparams(("parallel",)),
            name=f"moba_{qb}",
        )(y3, y3, y3))
    return jnp.stack(outs, axis=1).reshape(B * L, MOBA_HEADS * HEAD_DIM)


INT_MIN = -2 ** 31
NEG_INF_KEY = -2139095041
DSA_ROW_BLOCK = 128


DSA_ELEMS = 2
DSA_MXU_COUNT_ROWS = 768


def _dsa_kernel(q_ref, k_ref, v_ref, iq_ref, ik_ref, misc_ref, o_ref, key_ref, sel_ref, *, qt, top, mxu_count):
    E, TQ = q_ref.shape[0], q_ref.shape[1]
    S = k_ref.shape[1]
    kf = float(top)

    def indexer(e):
        iq_t = iq_ref[e].T.astype(BF16)
        w_t = misc_ref[e].T[MISC_W0:MISC_W0 + IDX_HEADS, :] * IDX_HEADS ** -0.5
        q_pairs = [iq_t[p * LANES:(p + 1) * LANES, :] for p in range(IDX_HEADS // 2)]
        w_rows = [w_t[h:h + 1, :] for h in range(IDX_HEADS)]
        for j in range(S // DSA_ROW_BLOCK):
            rows = slice(j * DSA_ROW_BLOCK, (j + 1) * DSA_ROW_BLOCK)
            ik = ik_ref[e, rows, :]
            ik_lo = ik.astype(BF16)
            ik_hi = pltpu.roll(ik, IDX_DIM, 1).astype(BF16)
            score = None
            for p, q_pair in enumerate(q_pairs):
                for half, ik_half in enumerate((ik_lo, ik_hi)):
                    term = jnp.maximum(_dot(ik_half, q_pair), 0.0) * w_rows[2 * p + half]
                    score = term if score is None else score + term
            score = jnp.where(score == 0.0, 0.0, score)
            if (j + 1) * DSA_ROW_BLOCK > qt * TQ + 1:
                krow = lax.broadcasted_iota(jnp.int32, score.shape, 0) + j * DSA_ROW_BLOCK
                qpos = lax.broadcasted_iota(jnp.int32, score.shape, 1) + qt * TQ
                score = jnp.where(krow <= qpos, score, -jnp.inf)
            bits = pltpu.bitcast(score, jnp.int32)
            key_ref[e, rows, :] = bits ^ ((bits >> 31) & 0x7FFFFFFF)

    ones = jnp.ones((8, S), BF16)

    def count_ge(e, t):
        hit = jnp.where(key_ref[e] >= t, 1.0, 0.0)
        if mxu_count:
            return _dot(ones, hit.astype(BF16))[0:1, :]
        return jnp.sum(hit, axis=0, keepdims=True)

    for e in range(E):
        indexer(e)

    state = []
    for e in range(E):
        n_nonneg = count_ge(e, jnp.zeros((1, TQ), jnp.int32))
        start = n_nonneg >= kf
        state += [jnp.where(start, 0, INT_MIN).astype(jnp.int32), jnp.where(start, n_nonneg, float(S))]

    def bit_step(i, state):
        out = []
        for e in range(E):
            prefix, n_ge = state[2 * e], state[2 * e + 1]
            cand = prefix | jnp.left_shift(jnp.int32(1), 30 - i)
            n_cand = count_ge(e, cand)
            take = n_cand >= kf
            out += [jnp.where(take, cand, prefix), jnp.where(take, n_cand, n_ge)]
        return tuple(out)

    state = lax.fori_loop(0, 31, bit_step, tuple(state))

    for e in range(E):
        _dsa_select_and_attend(e, state[2 * e], state[2 * e + 1], q_ref, k_ref, v_ref, o_ref, key_ref, sel_ref,
                               qt=qt, kf=kf)


def _dsa_select_and_attend(e, thr, n_ge, q_ref, k_ref, v_ref, o_ref, key_ref, sel_ref, *, qt, kf):
    TQ, S = q_ref.shape[1], k_ref.shape[1]
    real = thr > NEG_INF_KEY
    thr_sel = jnp.maximum(thr, NEG_INF_KEY + 1)
    sel_ref[e] = jnp.where(key_ref[e] >= thr_sel, 0.0, -jnp.inf)

    @pl.when(jnp.max(jnp.where(real, n_ge, 0.0)) > kf)
    def _():
        n_gt = jnp.sum(jnp.where(key_ref[e] > thr, 1.0, 0.0), axis=0, keepdims=True)
        r = lax.broadcasted_iota(jnp.int32, (TQ, TQ), 0)
        c = lax.broadcasted_iota(jnp.int32, (TQ, TQ), 1)
        before = jnp.where(c < r, 1.0, 0.0).astype(BF16)
        room = kf - n_gt
        seen = jnp.zeros_like(n_gt)
        for i in range(S // TQ):
            rows = slice(i * TQ, (i + 1) * TQ)
            k_i = key_ref[e, rows, :]
            eq_i = k_i == thr
            eq_f = jnp.where(eq_i, 1.0, 0.0)
            n_eq_before = seen + _dot(before, eq_f.astype(BF16))
            keep = (k_i > thr) | (eq_i & (n_eq_before < room))
            valid = (r + i * TQ) <= (c + qt * TQ)
            sel_ref[e, rows, :] = jnp.where(keep & valid, 0.0, -jnp.inf)
            seen = seen + jnp.sum(eq_f, axis=0, keepdims=True)

    q_t = q_ref[e].T
    feat_head = lax.broadcasted_iota(jnp.int32, q_t.shape, 0) // HEAD_DIM
    kb = k_ref[e].astype(BF16)
    v_t = v_ref[e].T.astype(BF16)
    outs = []
    for h in range(DSA_HEADS):
        qth = jnp.where(feat_head == h, q_t, 0.0).astype(BF16)
        s = _dot(kb, qth) + sel_ref[e]
        m = jnp.max(s, axis=0, keepdims=True)
        p = jnp.exp(s - m)
        den = jnp.sum(p, axis=0, keepdims=True)
        outs.append(_dot(v_t[h * HEAD_DIM:(h + 1) * HEAD_DIM, :], p.astype(BF16)) / den)
    o_ref[e] = jnp.concatenate(outs, axis=0).T


def _dsa(y3, tq=256):
    B, L, _ = y3.shape
    top = min(DSA_TOPK, L // 4)
    E = DSA_ELEMS if B % DSA_ELEMS == 0 else 1
    outs = []
    for qt in range(L // tq):
        S = (qt + 1) * tq
        outs.append(pl.pallas_call(
            functools.partial(_dsa_kernel, qt=qt, top=top, mxu_count=E > 1 and S >= DSA_MXU_COUNT_ROWS),
            grid=(B // E,),
            in_specs=[pl.BlockSpec((E, tq, 256), lambda b, qt=qt: (b, qt, _DST["s_q"][0] // 256)),
                      pl.BlockSpec((E, S, 256), lambda b: (b, 0, _DST["s_k"][0] // 256)),
                      pl.BlockSpec((E, S, 256), lambda b: (b, 0, _DST["s_v"][0] // 256)),
                      pl.BlockSpec((E, tq, 512), lambda b, qt=qt: (b, qt, _DST["i_q"][0] // 512)),
                      pl.BlockSpec((E, S, LANES), lambda b: (b, 0, _DST["i_k"][0] // LANES)),
                      pl.BlockSpec((E, tq, LANES), lambda b, qt=qt: (b, qt, _DST["misc"][0] // LANES))],
            out_specs=pl.BlockSpec((E, tq, 256), lambda b: (b, 0, 0)),
            out_shape=jax.ShapeDtypeStruct((B, tq, 256), F32),
            scratch_shapes=[pltpu.VMEM((E, S, tq), jnp.int32), pltpu.VMEM((E, S, tq), F32)],
            compiler_params=_cparams(("parallel",)),
            name=f"dsa_{qt}",
        )(y3, y3, y3, y3, y3, y3))
    return jnp.stack(outs, axis=1).reshape(B * L, DSA_HEADS * HEAD_DIM)


def _out_proj_kernel(h_ref, a_ref, b_ref, c_ref, w_ref, g_ref, beta_ref, rwh_ref, rwl_ref, rb_ref,
                     h1_ref, cls_ref, rank_ref, cnt_ref, carry_ref, *, alpha):
    @pl.when(pl.program_id(0) == 0)
    def _():
        carry_ref[...] = jnp.zeros_like(carry_ref)

    D = h_ref.shape[1]
    tm = h_ref.shape[0]
    wa, wb = a_ref.shape[1], b_ref.shape[1]
    mix = _dot(a_ref[...].astype(BF16), w_ref[0:wa, :])
    mix = mix + _dot(b_ref[...].astype(BF16), w_ref[wa:wa + wb, :])
    mix = mix + _dot(c_ref[...].astype(BF16), w_ref[wa + wb:, :])
    h1 = _layer_norm(alpha * h_ref[...] + mix, g_ref[...], beta_ref[...])
    h1_ref[:, 0:D] = h1
    h1_hi = h1.astype(BF16)
    h1_lo = (h1 - h1_hi.astype(F32)).astype(BF16)
    logits = _dot(h1_hi, rwh_ref[...]) + _dot(h1_lo, rwh_ref[...]) + _dot(h1_hi, rwl_ref[...])
    logits_t = logits.T[0:N_EXPERTS, :]
    scores = 1.0 / (1.0 + jnp.exp(-logits_t))
    sel = scores + rb_ref[...]
    rows = [sel[e:e + 1, :] for e in range(N_EXPERTS)]
    best_score, best = None, None
    for g in range(N_GROUPS):
        r = rows[g * EXPERTS_PER_GROUP:(g + 1) * EXPERTS_PER_GROUP]
        gs = None
        for i in range(EXPERTS_PER_GROUP):
            for j in range(i + 1, EXPERTS_PER_GROUP):
                pair = r[i] + r[j]
                gs = pair if gs is None else jnp.maximum(gs, pair)
        if g == 0:
            best_score, best = gs, jnp.zeros(gs.shape, jnp.int32)
        else:
            better = gs > best_score
            best = jnp.where(better, g, best)
            best_score = jnp.where(better, gs, best_score)
    expert = lax.broadcasted_iota(jnp.int32, sel.shape, 0)
    in_group = (expert // EXPERTS_PER_GROUP) == best
    masked = jnp.where(in_group, sel, -jnp.inf)
    rank = jnp.zeros(sel.shape, F32)
    for e in range(N_EXPERTS):
        other = masked[e:e + 1, :]
        ahead = (other > masked) | ((other == masked) & (e < expert))
        rank = rank + jnp.where(ahead, 1.0, 0.0)
    picked = jnp.where((rank < TOP_K) & in_group, 1.0, 0.0)
    chosen = picked * scores
    comb_t = chosen / jnp.sum(chosen, axis=0, keepdims=True)
    G = EXPERTS_PER_GROUP
    m = [sum(picked[g * G + i:g * G + i + 1, :] for g in range(N_GROUPS)) for i in range(G)]
    cw = [sum(comb_t[g * G + i:g * G + i + 1, :] for g in range(N_GROUPS)) for i in range(G)]
    pair = jnp.zeros_like(m[0])
    for idx, (i, j) in enumerate(_PAIRS):
        pair = pair + float(idx) * m[i] * m[j]
    cls = best * len(_PAIRS) + pair.astype(jnp.int32)
    w_lo = jnp.where(m[0] > 0, cw[0], jnp.where(m[1] > 0, cw[1], cw[2]))
    w_hi = jnp.where(m[3] > 0, cw[3], jnp.where(m[2] > 0, cw[2], cw[1]))
    pad = jnp.zeros((LANES - 2, tm), F32)
    h1_ref[:, D:D + LANES] = jnp.concatenate([w_lo, w_hi, pad], axis=0).T
    onehot = jnp.where(lax.broadcasted_iota(jnp.int32, (CLASS_ROWS, tm), 0) == cls, 1.0, 0.0)
    r = lax.broadcasted_iota(jnp.int32, (tm, tm), 0)
    c = lax.broadcasted_iota(jnp.int32, (tm, tm), 1)
    earlier = jnp.where(r < c, 1.0, 0.0).astype(BF16)
    before = _dot(onehot.astype(BF16), earlier) + carry_ref[...]
    cls_ref[...] = cls
    rank_ref[...] = jnp.sum(onehot * before, axis=0, keepdims=True).astype(jnp.int32)
    carry = carry_ref[...] + jnp.sum(onehot, axis=1, keepdims=True)
    carry_ref[...] = carry
    cnt_ref[...] = carry[:, 0:LANES]


def _out_proj(h, out_a, out_b, out_c, w_out, ln_g, ln_b, router_w, router_bias, alpha, tm=512):
    T, D = h.shape
    nt = T // tm
    rw = jnp.zeros((D, LANES), F32).at[:, :N_EXPERTS].set(router_w)
    rw_hi = rw.astype(BF16)
    rw_lo = (rw - rw_hi.astype(F32)).astype(BF16)
    rb = jnp.broadcast_to(router_bias.astype(F32).reshape(N_EXPERTS, 1), (N_EXPERTS, tm))
    row = lambda i: (i, 0)
    full = lambda i: (0, 0)
    vec = pl.BlockSpec((None, 1, tm), lambda i: (i, 0, 0))
    return pl.pallas_call(
        functools.partial(_out_proj_kernel, alpha=alpha),
        grid=(nt,),
        in_specs=[pl.BlockSpec((tm, D), row),
                  pl.BlockSpec((tm, out_a.shape[1]), row),
                  pl.BlockSpec((tm, out_b.shape[1]), row),
                  pl.BlockSpec((tm, out_c.shape[1]), row),
                  pl.BlockSpec(w_out.shape, full),
                  pl.BlockSpec((1, D), full), pl.BlockSpec((1, D), full),
                  pl.BlockSpec((D, LANES), full), pl.BlockSpec((D, LANES), full),
                  pl.BlockSpec((N_EXPERTS, tm), full)],
        out_specs=[pl.BlockSpec((tm, D + LANES), row), vec, vec, pl.BlockSpec((CLASS_ROWS, LANES), full)],
        out_shape=[jax.ShapeDtypeStruct((T, D + LANES), F32),
                   jax.ShapeDtypeStruct((nt, 1, tm), jnp.int32),
                   jax.ShapeDtypeStruct((nt, 1, tm), jnp.int32),
                   jax.ShapeDtypeStruct((CLASS_ROWS, LANES), F32)],
        scratch_shapes=[pltpu.VMEM((CLASS_ROWS, tm), F32)],
        compiler_params=_cparams(("arbitrary",)),
        name="out_proj",
    )(h, out_a, out_b, out_c, w_out.astype(BF16), ln_g.reshape(1, D), ln_b.reshape(1, D), rw_hi, rw_lo, rb)


def _plan(cls, rank, counts, tile):
    n_tiles = cls.size // tile + N_CLASSES
    cnt = counts[:N_CLASSES, 0].astype(jnp.int32)
    tiles_per_class = (cnt + tile - 1) // tile
    tile_end = jnp.cumsum(tiles_per_class)
    row_start = (tile_end - tiles_per_class) * tile
    pos = (row_start[cls.reshape(-1)] + rank.reshape(-1)).astype(jnp.int32)
    t = jnp.arange(n_tiles, dtype=jnp.int32)
    tile_cls = jnp.minimum(jnp.sum((t[:, None] >= tile_end[None, :]).astype(jnp.int32), axis=1), N_CLASSES - 1)
    used = (t < tile_end[-1]).astype(jnp.int32)
    lo = jnp.array([p[0] for p in _PAIRS], jnp.int32)
    hi = jnp.array([p[1] for p in _PAIRS], jnp.int32)
    group = tile_cls // len(_PAIRS)
    e_lo = group * EXPERTS_PER_GROUP + lo[tile_cls % len(_PAIRS)]
    e_hi = group * EXPERTS_PER_GROUP + hi[tile_cls % len(_PAIRS)]
    return pos, e_lo, e_hi, used, n_tiles


def _row_copies(pos_hbm, pos_smem, idx_sem, n_rows, make_copy):
    base = pl.multiple_of(pl.program_id(0) * n_rows, n_rows)
    idx_copy = pltpu.make_async_copy(pos_hbm.at[pl.ds(base, n_rows)], pos_smem, idx_sem)
    idx_copy.start()
    idx_copy.wait()

    def issue(r, carry):
        make_copy(r, pos_smem[r]).start()
        return carry

    lax.fori_loop(0, n_rows, issue, 0, unroll=8)

    def drain(r, carry):
        make_copy(r, pos_smem[r]).wait()
        return carry

    lax.fori_loop(0, n_rows, drain, 0, unroll=8)


def _dispatch_kernel(pos_hbm, x_ref, zero_hbm, xs_hbm, pos_smem, idx_sem, sem):
    del zero_hbm
    n_rows = x_ref.shape[0]
    _row_copies(pos_hbm, pos_smem, idx_sem, n_rows,
                lambda r, p: pltpu.make_async_copy(x_ref.at[r], xs_hbm.at[p], sem))


def _dispatch(x_aug, pos, n_rows_sorted, tm=2048):
    T, W = x_aug.shape
    return pl.pallas_call(
        _dispatch_kernel,
        grid=(T // tm,),
        in_specs=[pl.BlockSpec(memory_space=pl.ANY),
                  pl.BlockSpec((tm, W), lambda i: (i, 0)),
                  pl.BlockSpec(memory_space=pl.ANY)],
        out_specs=pl.BlockSpec(memory_space=pl.ANY),
        out_shape=jax.ShapeDtypeStruct((n_rows_sorted, W), F32),
        scratch_shapes=[pltpu.SMEM((tm,), jnp.int32), pltpu.SemaphoreType.DMA(()), pltpu.SemaphoreType.DMA(())],
        input_output_aliases={2: 0},
        compiler_params=_cparams(("arbitrary",)),
        name="moe_dispatch",
    )(pos, x_aug, jnp.zeros((n_rows_sorted, W), F32))


def _combine_kernel(pos_hbm, ys_hbm, o_ref, pos_smem, idx_sem, sem):
    n_rows = o_ref.shape[0]
    _row_copies(pos_hbm, pos_smem, idx_sem, n_rows,
                lambda r, p: pltpu.make_async_copy(ys_hbm.at[p], o_ref.at[r], sem))


def _combine(ys, pos, tm=2048):
    T = pos.shape[0]
    D = ys.shape[1]
    return pl.pallas_call(
        _combine_kernel,
        grid=(T // tm,),
        in_specs=[pl.BlockSpec(memory_space=pl.ANY), pl.BlockSpec(memory_space=pl.ANY)],
        out_specs=pl.BlockSpec((tm, D), lambda i: (i, 0)),
        out_shape=jax.ShapeDtypeStruct((T, D), F32),
        scratch_shapes=[pltpu.SMEM((tm,), jnp.int32), pltpu.SemaphoreType.DMA(()), pltpu.SemaphoreType.DMA(())],
        compiler_params=_cparams(("arbitrary",)),
        name="moe_combine",
    )(pos, ys)


def _moe_kernel(e_lo_ref, e_hi_ref, used_ref, xs_ref, wg1_ref, wu1_ref, wd1_ref, wg2_ref, wu2_ref, wd2_ref,
                g_ref, beta_ref, ys_ref, *, alpha):
    del e_lo_ref, e_hi_ref
    D = ys_ref.shape[1]
    used = used_ref[pl.program_id(0)] > 0

    @pl.when(used)
    def _():
        x = xs_ref[:, 0:D]
        w = xs_ref[:, D:D + LANES]
        xb = x.astype(BF16)

        def expert(wg_ref, wu_ref, wd_ref):
            act = _silu(_dot(xb, wg_ref[...])) * _dot(xb, wu_ref[...])
            return _dot(act.astype(BF16), wd_ref[...])

        ffn = w[:, 0:1] * expert(wg1_ref, wu1_ref, wd1_ref) + w[:, 1:2] * expert(wg2_ref, wu2_ref, wd2_ref)
        ys_ref[...] = _layer_norm(alpha * x + ffn, g_ref[...], beta_ref[...])

    @pl.when(jnp.logical_not(used))
    def _():
        ys_ref[...] = jnp.zeros_like(ys_ref)


def _moe(xs, e_lo, e_hi, used, w_gate, w_up, w_down, ln_g, ln_b, alpha, tile):
    R, W = xs.shape
    E, D, F = w_gate.shape
    first = lambda i, lo, hi, u: (lo[i], 0, 0)
    second = lambda i, lo, hi, u: (hi[i], 0, 0)
    const = lambda i, lo, hi, u: (0, 0)
    wg, wu, wd = w_gate.astype(BF16), w_up.astype(BF16), w_down.astype(BF16)
    grid_spec = pltpu.PrefetchScalarGridSpec(
        num_scalar_prefetch=3,
        grid=(R // tile,),
        in_specs=[pl.BlockSpec((tile, W), lambda i, lo, hi, u: (i, 0)),
                  pl.BlockSpec((None, D, F), first), pl.BlockSpec((None, D, F), first),
                  pl.BlockSpec((None, F, D), first),
                  pl.BlockSpec((None, D, F), second), pl.BlockSpec((None, D, F), second),
                  pl.BlockSpec((None, F, D), second),
                  pl.BlockSpec((1, D), const), pl.BlockSpec((1, D), const)],
        out_specs=pl.BlockSpec((tile, D), lambda i, lo, hi, u: (i, 0)),
    )
    return pl.pallas_call(
        functools.partial(_moe_kernel, alpha=alpha),
        grid_spec=grid_spec,
        out_shape=jax.ShapeDtypeStruct((R, D), F32),
        compiler_params=_cparams(("arbitrary",)),
        name="moe",
    )(e_lo, e_hi, used, xs, wg, wu, wd, wg, wu, wd, ln_g.reshape(1, D), ln_b.reshape(1, D))


def _moe_block(h1_aug, cls, rank, counts, w_gate, w_up, w_down, ln_g, ln_b, alpha, tile=MOE_TILE):
    pos, e_lo, e_hi, used, n_tiles = _plan(cls, rank, counts, tile)
    xs = _dispatch(h1_aug, pos, n_tiles * tile)
    ys = _moe(xs, e_lo, e_hi, used, w_gate, w_up, w_down, ln_g, ln_b, alpha, tile)
    return _combine(ys, pos)


def kernel(x, positions, w_in, gla_gate_up, gla_gate_bias, gla_norm_gain, w_out, ln_mix_g, ln_mix_b, router_w, router_bias, w_expert_gate, w_expert_up, w_expert_down, ln_ffn_g, ln_ffn_b):
    B, L, D = x.shape
    depth = w_in.shape[0]
    alpha = (2 * depth) ** 0.25
    T = B * L
    tabs = _rope_tables(positions)
    h = x.reshape(T, D)
    for layer in range(depth):
        y = _in_proj(h, _pack_w_in(w_in[layer]), tabs)
        out_a = _gla(y, B, L, gla_gate_up[layer], gla_gate_bias[layer], gla_norm_gain[layer])
        y3 = y.reshape(B, L, NP)
        out_b = _moba(y3)
        out_c = _dsa(y3)
        h1_aug, cls, rank, counts = _out_proj(h, out_a, out_b, out_c, w_out[layer], ln_mix_g[layer],
                                              ln_mix_b[layer], router_w, router_bias, alpha)
        h = _moe_block(h1_aug, cls, rank, counts, w_expert_gate[layer], w_expert_up[layer],
                       w_expert_down[layer], ln_ffn_g[layer], ln_ffn_b[layer], alpha)
    return h.reshape(B, L, D)
```

```python
import functools

import jax
import jax.numpy as jnp
from jax import lax
from jax.experimental import pallas as pl
from jax.experimental.pallas import tpu as pltpu

F32 = jnp.float32
BF16 = jnp.bfloat16

HEAD_DIM = 64
GLA_HEADS, GLA_DK, GLA_DV = 4, 64, 128
GLA_GATE_RANK = 16
GLA_GATE_NORM = 16.0
GLA_CHUNK = 64
MOBA_HEADS, MOBA_BLOCK, MOBA_TOPK = 4, 256, 3
DSA_HEADS, DSA_TOPK = 4, 256
IDX_HEADS, IDX_DIM = 8, 64
ROPE_THETA = 500000.0
ROPE_HALF = 8
N_EXPERTS, N_GROUPS, TOP_K = 16, 4, 2
EXPERTS_PER_GROUP = N_EXPERTS // N_GROUPS
_PAIRS = tuple((i, j) for i in range(EXPERTS_PER_GROUP) for j in range(i + 1, EXPERTS_PER_GROUP))
N_CLASSES = N_GROUPS * len(_PAIRS)
CLASS_ROWS = 32
MOE_TILE = 512
LN_EPS = 1e-5
RMS_EPS = 1e-6

LANES = 128
VMEM_LIMIT = 56 * 1024 * 1024

_SRC = dict(g_q=(0, 256), g_k=(256, 256), g_v=(512, 512), g_rank=(1024, 16), g_gate=(1040, 512),
            m_q=(1552, 256), m_k=(1808, 256), m_v=(2064, 256),
            s_q=(2320, 256), s_k=(2576, 256), s_v=(2832, 256),
            i_q=(3088, 512), i_k=(3600, 64), i_w=(3664, 8))
_DST = dict(g_q=(0, 256), g_k=(256, 256), g_v=(512, 512), g_gate=(1024, 512),
            m_q=(1536, 256), m_k=(1792, 256), m_v=(2048, 256),
            s_q=(2304, 256), s_k=(2560, 256), s_v=(2816, 256),
            i_q=(3072, 512), i_k=(3584, 128), misc=(3712, 128))
NP = 3840
MISC_RANK0 = 0
MISC_W0 = 16
_ROPED = ("m_q", "m_k", "s_q", "s_k", "i_q", "i_k")
_SCALED = dict(g_q=GLA_DK ** -0.5, m_q=HEAD_DIM ** -0.5, s_q=HEAD_DIM ** -0.5, i_q=IDX_DIM ** -0.5)


def _cparams(sem):
    return pltpu.CompilerParams(dimension_semantics=sem, vmem_limit_bytes=VMEM_LIMIT)


def _dot(a, b):
    return jnp.dot(a, b, preferred_element_type=F32)


def _dot_split(a, b):
    a_hi, a_lo = _split_bf16(a, 2)
    b_hi, b_lo = _split_bf16(b, 2)
    return _dot(a_hi, b_hi) + _dot(a_lo, b_hi) + _dot(a_hi, b_lo)


def _split_bf16(x, terms):
    parts = []
    for _ in range(terms - 1):
        p = x.astype(BF16)
        parts.append(p)
        x = x - p.astype(F32)
    parts.append(x.astype(BF16))
    return parts


def _dot_nt(a, b):
    return lax.dot_general(a, b, (((1,), (1,)), ((), ())), preferred_element_type=F32)


def _dot_tn(a, b):
    return lax.dot_general(a, b, (((0,), (0,)), ((), ())), preferred_element_type=F32)


def _layer_norm(x, g, b):
    mu = jnp.mean(x, -1, keepdims=True)
    xc = x - mu
    var = jnp.mean(xc * xc, -1, keepdims=True)
    return xc * lax.rsqrt(var + LN_EPS) * g + b


def _silu(x):
    return x * (1.0 / (1.0 + jnp.exp(-x)))


def _rope_table_kernel(pos_ref, inv_ref, c_ref, s1_ref, s2_ref):
    ang = pos_ref[...].astype(F32) * inv_ref[...]
    lane = lax.broadcasted_iota(jnp.int32, ang.shape, 1) % HEAD_DIM
    cos, sin = jnp.cos(ang), jnp.sin(ang)
    first = lane < ROPE_HALF
    second = (lane >= ROPE_HALF) & (lane < 2 * ROPE_HALF)
    c_ref[...] = jnp.where(first | second, cos, 1.0)
    s1_ref[...] = jnp.where(first, -sin, 0.0)
    s2_ref[...] = jnp.where(second, sin, 0.0)


def _rope_tables(positions, tm=512):
    T = positions.size
    pos = positions.reshape(T, 1)
    j = (jnp.arange(LANES) % ROPE_HALF).astype(F32)
    inv = (ROPE_THETA ** (-j / ROPE_HALF)).reshape(1, LANES)
    out = jax.ShapeDtypeStruct((T, LANES), F32)
    return pl.pallas_call(
        _rope_table_kernel,
        grid=(T // tm,),
        in_specs=[pl.BlockSpec((tm, 1), lambda i: (i, 0)), pl.BlockSpec((1, LANES), lambda i: (0, 0))],
        out_specs=[pl.BlockSpec((tm, LANES), lambda i: (i, 0))] * 3,
        out_shape=[out, out, out],
        compiler_params=_cparams(("parallel",)),
        name="rope_tables",
    )(pos, inv)


def _in_proj_kernel(h_ref, w_ref, c_ref, s1_ref, s2_ref, y_ref):
    hb = h_ref[...].astype(BF16)
    c, s1, s2 = c_ref[...], s1_ref[...], s2_ref[...]
    for name, (start, width) in _DST.items():
        y = _dot(hb, w_ref[:, start:start + width])
        if name in _SCALED:
            y = y * _SCALED[name]
        if name in _ROPED:
            blocks = []
            for o in range(0, width, LANES):
                yb = y[:, o:o + LANES]
                up = pltpu.roll(yb, LANES - ROPE_HALF, 1)
                down = pltpu.roll(yb, ROPE_HALF, 1)
                blocks.append(yb * c + up * s1 + down * s2)
            y = blocks[0] if len(blocks) == 1 else jnp.concatenate(blocks, axis=1)
        y_ref[:, start:start + width] = y


def _pack_w_in(w_in):
    D = w_in.shape[0]
    cols = []
    for name, (start, width) in _DST.items():
        if name == "misc":
            r0, rw = _SRC["g_rank"]
            w0, ww = _SRC["i_w"]
            blk = jnp.zeros((D, width), w_in.dtype)
            blk = blk.at[:, MISC_RANK0:MISC_RANK0 + rw].set(w_in[:, r0:r0 + rw])
            blk = blk.at[:, MISC_W0:MISC_W0 + ww].set(w_in[:, w0:w0 + ww])
        else:
            s0, sw = _SRC[name]
            blk = w_in[:, s0:s0 + sw]
            if sw < width:
                blk = jnp.pad(blk, ((0, 0), (0, width - sw)))
        cols.append(blk)
    return jnp.concatenate(cols, axis=1).astype(BF16)


def _in_proj(h, w_packed, tabs, tm=512):
    T, D = h.shape
    tab_spec = pl.BlockSpec((tm, LANES), lambda i: (i, 0))
    return pl.pallas_call(
        _in_proj_kernel,
        grid=(T // tm,),
        in_specs=[pl.BlockSpec((tm, D), lambda i: (i, 0)),
                  pl.BlockSpec((D, NP), lambda i: (0, 0)),
                  tab_spec, tab_spec, tab_spec],
        out_specs=pl.BlockSpec((tm, NP), lambda i: (i, 0)),
        out_shape=jax.ShapeDtypeStruct((T, NP), F32),
        compiler_params=_cparams(("parallel",)),
        name="in_proj",
    )(h, w_packed, *tabs)


def _gla_kernel(q_ref, k_ref, v_ref, g_ref, misc_ref, up_ref, bias_ref, gain_ref, o_ref, st_ref, *, tl):
    @pl.when(pl.program_id(1) == 0)
    def _():
        st_ref[...] = jnp.zeros_like(st_ref)

    C = GLA_CHUNK
    dkw, dvw = GLA_HEADS * GLA_DK, GLA_HEADS * GLA_DV
    z = _dot_split(misc_ref[...], up_ref[...]) + bias_ref[...]
    log_a = (jnp.minimum(z, 0.0) - jnp.log(1.0 + jnp.exp(-jnp.abs(z)))) / GLA_GATE_NORM
    log_a_terms = _split_bf16(log_a, 3)
    ri = lax.broadcasted_iota(jnp.int32, (C, C), 0)
    ci = lax.broadcasted_iota(jnp.int32, (C, C), 1)
    causal = ci <= ri
    tri = jnp.where(causal, 1.0, 0.0).astype(BF16)
    head_rows = lax.broadcasted_iota(jnp.int32, (GLA_HEADS * C, dkw), 0) // C
    head_lanes = lax.broadcasted_iota(jnp.int32, (GLA_HEADS * C, dkw), 1) // GLA_DK
    own_head = head_rows == head_lanes
    sr = lax.broadcasted_iota(jnp.int32, (dvw, dkw), 0) // GLA_DV
    sc = lax.broadcasted_iota(jnp.int32, (dvw, dkw), 1) // GLA_DK
    diag = sr == sc
    gain = gain_ref[...]

    for c in range(tl // C):
        rows = slice(c * C, (c + 1) * C)
        b = sum(_dot(tri, term[rows]) for term in log_a_terms)
        b_last = b[C - 1:C, :]
        q_dec = q_ref[rows, :] * jnp.exp(b)
        k_c = k_ref[rows, :]
        k_inv = (k_c * jnp.exp(-b)).astype(BF16)
        k_dec = (k_c * jnp.exp(b_last - b)).astype(BF16)
        v_c = v_ref[rows, :].astype(BF16)
        q_dec_b = q_dec.astype(BF16)
        st = st_ref[...]
        o = _dot_nt(q_dec_b, st.astype(BF16))
        q_heads = jnp.where(own_head, jnp.concatenate([q_dec] * GLA_HEADS, axis=0), 0.0).astype(BF16)
        a_all = _dot_nt(q_heads, k_inv)
        intra = []
        for h in range(GLA_HEADS):
            a = jnp.where(causal, a_all[h * C:(h + 1) * C, :], 0.0)
            intra.append(_dot(a.astype(BF16), v_c[:, h * GLA_DV:(h + 1) * GLA_DV]))
        o = o + jnp.concatenate(intra, axis=1)
        kv_t = _dot_tn(v_c, k_dec)
        st_ref[...] = jnp.exp(b_last) * st + jnp.where(diag, kv_t, 0.0)
        outs = []
        for h in range(GLA_HEADS):
            oh = o[:, h * GLA_DV:(h + 1) * GLA_DV]
            ms = jnp.mean(oh * oh, -1, keepdims=True)
            outs.append(oh * lax.rsqrt(ms + RMS_EPS) * gain)
        o_ref[rows, :] = jnp.concatenate(outs, axis=1) * _silu(g_ref[rows, :])


def _gla(y, B, L, gate_up, gate_bias, norm_gain, tl=512):
    T = B * L
    nt = L // tl
    up = jnp.zeros((LANES, GLA_HEADS * GLA_DK), F32).at[MISC_RANK0:MISC_RANK0 + GLA_GATE_RANK].set(gate_up)
    row = lambda b, t: b * nt + t
    return pl.pallas_call(
        functools.partial(_gla_kernel, tl=tl),
        grid=(B, nt),
        in_specs=[pl.BlockSpec((tl, 256), lambda b, t: (row(b, t), _DST["g_q"][0] // 256)),
                  pl.BlockSpec((tl, 256), lambda b, t: (row(b, t), _DST["g_k"][0] // 256)),
                  pl.BlockSpec((tl, 512), lambda b, t: (row(b, t), _DST["g_v"][0] // 512)),
                  pl.BlockSpec((tl, 512), lambda b, t: (row(b, t), _DST["g_gate"][0] // 512)),
                  pl.BlockSpec((tl, LANES), lambda b, t: (row(b, t), _DST["misc"][0] // LANES)),
                  pl.BlockSpec((LANES, 256), lambda b, t: (0, 0)),
                  pl.BlockSpec((1, 256), lambda b, t: (0, 0)),
                  pl.BlockSpec((1, GLA_DV), lambda b, t: (0, 0))],
        out_specs=pl.BlockSpec((tl, 512), lambda b, t: (row(b, t), 0)),
        out_shape=jax.ShapeDtypeStruct((T, GLA_HEADS * GLA_DV), F32),
        scratch_shapes=[pltpu.VMEM((GLA_HEADS * GLA_DV, GLA_HEADS * GLA_DK), F32)],
        compiler_params=_cparams(("parallel", "arbitrary")),
        name="gla",
    )(y, y, y, y, y, up, gate_bias.reshape(1, -1), norm_gain.reshape(1, -1))


def _moba_kernel(q_ref, k_ref, v_ref, buf_ref, o_ref, *, qb):
    del buf_ref
    BLK = MOBA_BLOCK
    qt = q_ref[...].T
    feat_head = lax.broadcasted_iota(jnp.int32, qt.shape, 0) // HEAD_DIM
    krow = lax.broadcasted_iota(jnp.int32, (BLK, BLK), 0)
    qcol = lax.broadcasted_iota(jnp.int32, (BLK, BLK), 1)
    own_bias = jnp.where(krow <= qcol, 0.0, -jnp.inf)
    k_blocks = [k_ref[j * BLK:(j + 1) * BLK, :] for j in range(qb + 1)]
    kb_blocks = [kb.astype(BF16) for kb in k_blocks]
    vt_blocks = [v_ref[j * BLK:(j + 1) * BLK, :].T.astype(BF16) for j in range(qb + 1)]
    if qb > 0:
        k_mean = jnp.concatenate([jnp.mean(kb, axis=0, keepdims=True) for kb in k_blocks[:qb]], axis=0)
    outs = []
    for h in range(MOBA_HEADS):
        qth = jnp.where(feat_head == h, qt, 0.0)
        qth_b = qth.astype(BF16)
        biases = []
        if qb > 0:
            gate = _dot_split(k_mean, qth)
            g = [gate[j:j + 1, :] for j in range(qb)]
            for j in range(qb):
                rank = jnp.zeros_like(g[j])
                for i in range(qb):
                    if i == j:
                        continue
                    ahead = (g[i] >= g[j]) if i < j else (g[i] > g[j])
                    rank = rank + jnp.where(ahead, 1.0, 0.0)
                biases.append(jnp.where(rank < MOBA_TOPK, 0.0, -jnp.inf))
        s = [_dot(kb_blocks[j], qth_b) + biases[j] for j in range(qb)]
        s.append(_dot(kb_blocks[qb], qth_b) + own_bias)
        m = jnp.max(s[0], axis=0, keepdims=True)
        for sj in s[1:]:
            m = jnp.maximum(m, jnp.max(sj, axis=0, keepdims=True))
        den = jnp.zeros_like(m)
        acc = jnp.zeros((HEAD_DIM, BLK), F32)
        for j in range(qb + 1):
            p = jnp.exp(s[j] - m)
            den = den + jnp.sum(p, axis=0, keepdims=True)
            acc = acc + _dot(vt_blocks[j][h * HEAD_DIM:(h + 1) * HEAD_DIM, :], p.astype(BF16))
        outs.append(acc / den)
    o_ref[...] = jnp.concatenate(outs, axis=0).T


def _moba(y3):
    B, L, _ = y3.shape
    BLK = MOBA_BLOCK
    nb = L // BLK
    out = jnp.zeros((B, L, MOBA_HEADS * HEAD_DIM), F32)
    for qb in range(nb):
        S = (qb + 1) * BLK
        out = pl.pallas_call(
            functools.partial(_moba_kernel, qb=qb),
            grid=(B,),
            in_specs=[pl.BlockSpec((None, BLK, 256), lambda b, qb=qb: (b, qb, _DST["m_q"][0] // 256)),
                      pl.BlockSpec((None, S, 256), lambda b: (b, 0, _DST["m_k"][0] // 256)),
                      pl.BlockSpec((None, S, 256), lambda b: (b, 0, _DST["m_v"][0] // 256)),
                      pl.BlockSpec(memory_space=pl.ANY)],
            out_specs=pl.BlockSpec((None, BLK, 256), lambda b, qb=qb: (b, qb, 0)),
            out_shape=jax.ShapeDtypeStruct(out.shape, F32),
            input_output_aliases={3: 0},
            compiler_params=_cparams(("parallel",)),
            name=f"moba_{qb}",
        )(y3, y3, y3, out)
    return out.reshape(B * L, MOBA_HEADS * HEAD_DIM)


INT_MIN = -2 ** 31
NEG_INF_KEY = -2139095041
DSA_ROW_BLOCK = 128


DSA_ELEMS = 2
DSA_MXU_COUNT_ROWS = 768


def _dsa_kernel(q_ref, k_ref, v_ref, iq_ref, ik_ref, misc_ref, buf_ref, o_ref, key_ref, sel_ref, *, qt, top,
                mxu_count):
    del buf_ref
    E, TQ = q_ref.shape[0], q_ref.shape[1]
    S = k_ref.shape[1]
    kf = float(top)

    def indexer(e):
        iq_t = iq_ref[e].T.astype(BF16)
        w_t = misc_ref[e].T[MISC_W0:MISC_W0 + IDX_HEADS, :] * IDX_HEADS ** -0.5
        q_pairs = [iq_t[p * LANES:(p + 1) * LANES, :] for p in range(IDX_HEADS // 2)]
        w_rows = [w_t[h:h + 1, :] for h in range(IDX_HEADS)]
        for j in range(S // DSA_ROW_BLOCK):
            rows = slice(j * DSA_ROW_BLOCK, (j + 1) * DSA_ROW_BLOCK)
            ik = ik_ref[e, rows, :]
            ik_lo = ik.astype(BF16)
            ik_hi = pltpu.roll(ik, IDX_DIM, 1).astype(BF16)
            score = None
            for p, q_pair in enumerate(q_pairs):
                for half, ik_half in enumerate((ik_lo, ik_hi)):
                    term = jnp.maximum(_dot(ik_half, q_pair), 0.0) * w_rows[2 * p + half]
                    score = term if score is None else score + term
            score = jnp.where(score == 0.0, 0.0, score)
            if (j + 1) * DSA_ROW_BLOCK > qt * TQ + 1:
                krow = lax.broadcasted_iota(jnp.int32, score.shape, 0) + j * DSA_ROW_BLOCK
                qpos = lax.broadcasted_iota(jnp.int32, score.shape, 1) + qt * TQ
                score = jnp.where(krow <= qpos, score, -jnp.inf)
            bits = pltpu.bitcast(score, jnp.int32)
            key_ref[e, rows, :] = bits ^ ((bits >> 31) & 0x7FFFFFFF)

    ones = jnp.ones((8, S), BF16)

    def count_ge(e, t):
        hit = jnp.where(key_ref[e] >= t, 1.0, 0.0)
        if mxu_count:
            return _dot(ones, hit.astype(BF16))[0:1, :]
        return jnp.sum(hit, axis=0, keepdims=True)

    for e in range(E):
        indexer(e)

    state = []
    for e in range(E):
        n_nonneg = count_ge(e, jnp.zeros((1, TQ), jnp.int32))
        start = n_nonneg >= kf
        state += [jnp.where(start, 0, INT_MIN).astype(jnp.int32), jnp.where(start, n_nonneg, float(S))]

    def bit_step(i, state):
        out = []
        for e in range(E):
            prefix, n_ge = state[2 * e], state[2 * e + 1]
            cand = prefix | jnp.left_shift(jnp.int32(1), 30 - i)
            n_cand = count_ge(e, cand)
            take = n_cand >= kf
            out += [jnp.where(take, cand, prefix), jnp.where(take, n_cand, n_ge)]
        return tuple(out)

    state = lax.fori_loop(0, 31, bit_step, tuple(state))

    for e in range(E):
        _dsa_select_and_attend(e, state[2 * e], state[2 * e + 1], q_ref, k_ref, v_ref, o_ref, key_ref, sel_ref,
                               qt=qt, kf=kf)


def _dsa_select_and_attend(e, thr, n_ge, q_ref, k_ref, v_ref, o_ref, key_ref, sel_ref, *, qt, kf):
    TQ, S = q_ref.shape[1], k_ref.shape[1]
    real = thr > NEG_INF_KEY
    thr_sel = jnp.maximum(thr, NEG_INF_KEY + 1)
    sel_ref[e] = jnp.where(key_ref[e] >= thr_sel, 0.0, -jnp.inf)

    @pl.when(jnp.max(jnp.where(real, n_ge, 0.0)) > kf)
    def _():
        n_gt = jnp.sum(jnp.where(key_ref[e] > thr, 1.0, 0.0), axis=0, keepdims=True)
        r = lax.broadcasted_iota(jnp.int32, (TQ, TQ), 0)
        c = lax.broadcasted_iota(jnp.int32, (TQ, TQ), 1)
        before = jnp.where(c < r, 1.0, 0.0).astype(BF16)
        room = kf - n_gt
        seen = jnp.zeros_like(n_gt)
        for i in range(S // TQ):
            rows = slice(i * TQ, (i + 1) * TQ)
            k_i = key_ref[e, rows, :]
            eq_i = k_i == thr
            eq_f = jnp.where(eq_i, 1.0, 0.0)
            n_eq_before = seen + _dot(before, eq_f.astype(BF16))
            keep = (k_i > thr) | (eq_i & (n_eq_before < room))
            valid = (r + i * TQ) <= (c + qt * TQ)
            sel_ref[e, rows, :] = jnp.where(keep & valid, 0.0, -jnp.inf)
            seen = seen + jnp.sum(eq_f, axis=0, keepdims=True)

    q_t = q_ref[e].T
    feat_head = lax.broadcasted_iota(jnp.int32, q_t.shape, 0) // HEAD_DIM
    kb = k_ref[e].astype(BF16)
    v_t = v_ref[e].T.astype(BF16)
    outs = []
    for h in range(DSA_HEADS):
        qth = jnp.where(feat_head == h, q_t, 0.0).astype(BF16)
        s = _dot(kb, qth) + sel_ref[e]
        m = jnp.max(s, axis=0, keepdims=True)
        p = jnp.exp(s - m)
        den = jnp.sum(p, axis=0, keepdims=True)
        outs.append(_dot(v_t[h * HEAD_DIM:(h + 1) * HEAD_DIM, :], p.astype(BF16)) / den)
    o_ref[e] = jnp.concatenate(outs, axis=0).T


def _dsa(y3, tq=256):
    B, L, _ = y3.shape
    top = min(DSA_TOPK, L // 4)
    E = DSA_ELEMS if B % DSA_ELEMS == 0 else 1
    out = jnp.zeros((B, L, DSA_HEADS * HEAD_DIM), F32)
    for qt in range(L // tq):
        S = (qt + 1) * tq
        out = pl.pallas_call(
            functools.partial(_dsa_kernel, qt=qt, top=top, mxu_count=E > 1 and S >= DSA_MXU_COUNT_ROWS),
            grid=(B // E,),
            in_specs=[pl.BlockSpec((E, tq, 256), lambda b, qt=qt: (b, qt, _DST["s_q"][0] // 256)),
                      pl.BlockSpec((E, S, 256), lambda b: (b, 0, _DST["s_k"][0] // 256)),
                      pl.BlockSpec((E, S, 256), lambda b: (b, 0, _DST["s_v"][0] // 256)),
                      pl.BlockSpec((E, tq, 512), lambda b, qt=qt: (b, qt, _DST["i_q"][0] // 512)),
                      pl.BlockSpec((E, S, LANES), lambda b: (b, 0, _DST["i_k"][0] // LANES)),
                      pl.BlockSpec((E, tq, LANES), lambda b, qt=qt: (b, qt, _DST["misc"][0] // LANES)),
                      pl.BlockSpec(memory_space=pl.ANY)],
            out_specs=pl.BlockSpec((E, tq, 256), lambda b, qt=qt: (b, qt, 0)),
            out_shape=jax.ShapeDtypeStruct(out.shape, F32),
            scratch_shapes=[pltpu.VMEM((E, S, tq), jnp.int32), pltpu.VMEM((E, S, tq), F32)],
            input_output_aliases={6: 0},
            compiler_params=_cparams(("parallel",)),
            name=f"dsa_{qt}",
        )(y3, y3, y3, y3, y3, y3, out)
    return out.reshape(B * L, DSA_HEADS * HEAD_DIM)


def _out_proj_kernel(h_ref, a_ref, b_ref, c_ref, w_ref, g_ref, beta_ref, rwh_ref, rwl_ref, rb_ref,
                     h1_ref, cls_ref, rank_ref, cnt_ref, carry_ref, *, alpha):
    @pl.when(pl.program_id(0) == 0)
    def _():
        carry_ref[...] = jnp.zeros_like(carry_ref)

    D = h_ref.shape[1]
    tm = h_ref.shape[0]
    wa, wb = a_ref.shape[1], b_ref.shape[1]
    mix = _dot(a_ref[...].astype(BF16), w_ref[0:wa, :])
    mix = mix + _dot(b_ref[...].astype(BF16), w_ref[wa:wa + wb, :])
    mix = mix + _dot(c_ref[...].astype(BF16), w_ref[wa + wb:, :])
    h1 = _layer_norm(alpha * h_ref[...] + mix, g_ref[...], beta_ref[...])
    h1_ref[:, 0:D] = h1
    h1_hi = h1.astype(BF16)
    h1_lo = (h1 - h1_hi.astype(F32)).astype(BF16)
    logits = _dot(h1_hi, rwh_ref[...]) + _dot(h1_lo, rwh_ref[...]) + _dot(h1_hi, rwl_ref[...])
    logits_t = logits.T[0:N_EXPERTS, :]
    scores = 1.0 / (1.0 + jnp.exp(-logits_t))
    sel = scores + rb_ref[...]
    rows = [sel[e:e + 1, :] for e in range(N_EXPERTS)]
    best_score, best = None, None
    for g in range(N_GROUPS):
        r = rows[g * EXPERTS_PER_GROUP:(g + 1) * EXPERTS_PER_GROUP]
        gs = None
        for i in range(EXPERTS_PER_GROUP):
            for j in range(i + 1, EXPERTS_PER_GROUP):
                pair = r[i] + r[j]
                gs = pair if gs is None else jnp.maximum(gs, pair)
        if g == 0:
            best_score, best = gs, jnp.zeros(gs.shape, jnp.int32)
        else:
            better = gs > best_score
            best = jnp.where(better, g, best)
            best_score = jnp.where(better, gs, best_score)
    expert = lax.broadcasted_iota(jnp.int32, sel.shape, 0)
    in_group = (expert // EXPERTS_PER_GROUP) == best
    masked = jnp.where(in_group, sel, -jnp.inf)
    rank = jnp.zeros(sel.shape, F32)
    for e in range(N_EXPERTS):
        other = masked[e:e + 1, :]
        ahead = (other > masked) | ((other == masked) & (e < expert))
        rank = rank + jnp.where(ahead, 1.0, 0.0)
    picked = jnp.where((rank < TOP_K) & in_group, 1.0, 0.0)
    chosen = picked * scores
    comb_t = chosen / jnp.sum(chosen, axis=0, keepdims=True)
    G = EXPERTS_PER_GROUP
    m = [sum(picked[g * G + i:g * G + i + 1, :] for g in range(N_GROUPS)) for i in range(G)]
    cw = [sum(comb_t[g * G + i:g * G + i + 1, :] for g in range(N_GROUPS)) for i in range(G)]
    pair = jnp.zeros_like(m[0])
    for idx, (i, j) in enumerate(_PAIRS):
        pair = pair + float(idx) * m[i] * m[j]
    cls = best * len(_PAIRS) + pair.astype(jnp.int32)
    w_lo = jnp.where(m[0] > 0, cw[0], jnp.where(m[1] > 0, cw[1], cw[2]))
    w_hi = jnp.where(m[3] > 0, cw[3], jnp.where(m[2] > 0, cw[2], cw[1]))
    pad = jnp.zeros((LANES - 2, tm), F32)
    h1_ref[:, D:D + LANES] = jnp.concatenate([w_lo, w_hi, pad], axis=0).T
    onehot = jnp.where(lax.broadcasted_iota(jnp.int32, (CLASS_ROWS, tm), 0) == cls, 1.0, 0.0)
    r = lax.broadcasted_iota(jnp.int32, (tm, tm), 0)
    c = lax.broadcasted_iota(jnp.int32, (tm, tm), 1)
    earlier = jnp.where(r < c, 1.0, 0.0).astype(BF16)
    before = _dot(onehot.astype(BF16), earlier) + carry_ref[...]
    cls_ref[...] = cls
    rank_ref[...] = jnp.sum(onehot * before, axis=0, keepdims=True).astype(jnp.int32)
    carry = carry_ref[...] + jnp.sum(onehot, axis=1, keepdims=True)
    carry_ref[...] = carry
    cnt_ref[...] = carry[:, 0:LANES]


def _out_proj(h, out_a, out_b, out_c, w_out, ln_g, ln_b, router_w, router_bias, alpha, tm=512):
    T, D = h.shape
    nt = T // tm
    rw = jnp.zeros((D, LANES), F32).at[:, :N_EXPERTS].set(router_w)
    rw_hi = rw.astype(BF16)
    rw_lo = (rw - rw_hi.astype(F32)).astype(BF16)
    rb = jnp.broadcast_to(router_bias.astype(F32).reshape(N_EXPERTS, 1), (N_EXPERTS, tm))
    row = lambda i: (i, 0)
    full = lambda i: (0, 0)
    vec = pl.BlockSpec((None, 1, tm), lambda i: (i, 0, 0))
    return pl.pallas_call(
        functools.partial(_out_proj_kernel, alpha=alpha),
        grid=(nt,),
        in_specs=[pl.BlockSpec((tm, D), row),
                  pl.BlockSpec((tm, out_a.shape[1]), row),
                  pl.BlockSpec((tm, out_b.shape[1]), row),
                  pl.BlockSpec((tm, out_c.shape[1]), row),
                  pl.BlockSpec(w_out.shape, full),
                  pl.BlockSpec((1, D), full), pl.BlockSpec((1, D), full),
                  pl.BlockSpec((D, LANES), full), pl.BlockSpec((D, LANES), full),
                  pl.BlockSpec((N_EXPERTS, tm), full)],
        out_specs=[pl.BlockSpec((tm, D + LANES), row), vec, vec, pl.BlockSpec((CLASS_ROWS, LANES), full)],
        out_shape=[jax.ShapeDtypeStruct((T, D + LANES), F32),
                   jax.ShapeDtypeStruct((nt, 1, tm), jnp.int32),
                   jax.ShapeDtypeStruct((nt, 1, tm), jnp.int32),
                   jax.ShapeDtypeStruct((CLASS_ROWS, LANES), F32)],
        scratch_shapes=[pltpu.VMEM((CLASS_ROWS, tm), F32)],
        compiler_params=_cparams(("arbitrary",)),
        name="out_proj",
    )(h, out_a, out_b, out_c, w_out.astype(BF16), ln_g.reshape(1, D), ln_b.reshape(1, D), rw_hi, rw_lo, rb)


def _plan(cls, rank, counts, tile):
    n_tiles = cls.size // tile + N_CLASSES
    cnt = counts[:N_CLASSES, 0].astype(jnp.int32)
    tiles_per_class = (cnt + tile - 1) // tile
    tile_end = jnp.cumsum(tiles_per_class)
    row_start = (tile_end - tiles_per_class) * tile
    pos = (row_start[cls.reshape(-1)] + rank.reshape(-1)).astype(jnp.int32)
    t = jnp.arange(n_tiles, dtype=jnp.int32)
    tile_cls = jnp.minimum(jnp.sum((t[:, None] >= tile_end[None, :]).astype(jnp.int32), axis=1), N_CLASSES - 1)
    used = (t < tile_end[-1]).astype(jnp.int32)
    lo = jnp.array([p[0] for p in _PAIRS], jnp.int32)
    hi = jnp.array([p[1] for p in _PAIRS], jnp.int32)
    group = tile_cls // len(_PAIRS)
    e_lo = group * EXPERTS_PER_GROUP + lo[tile_cls % len(_PAIRS)]
    e_hi = group * EXPERTS_PER_GROUP + hi[tile_cls % len(_PAIRS)]
    return pos, e_lo, e_hi, used, n_tiles


def _row_copies(pos_hbm, pos_smem, idx_sem, n_rows, make_copy):
    base = pl.multiple_of(pl.program_id(0) * n_rows, n_rows)
    idx_copy = pltpu.make_async_copy(pos_hbm.at[pl.ds(base, n_rows)], pos_smem, idx_sem)
    idx_copy.start()
    idx_copy.wait()

    def issue(r, carry):
        make_copy(r, pos_smem[r]).start()
        return carry

    lax.fori_loop(0, n_rows, issue, 0, unroll=8)

    def drain(r, carry):
        make_copy(r, pos_smem[r]).wait()
        return carry

    lax.fori_loop(0, n_rows, drain, 0, unroll=8)


def _dispatch_kernel(pos_hbm, x_ref, zero_hbm, xs_hbm, pos_smem, idx_sem, sem):
    del zero_hbm
    n_rows = x_ref.shape[0]
    _row_copies(pos_hbm, pos_smem, idx_sem, n_rows,
                lambda r, p: pltpu.make_async_copy(x_ref.at[r], xs_hbm.at[p], sem))


def _dispatch(x_aug, pos, n_rows_sorted, tm=2048):
    T, W = x_aug.shape
    return pl.pallas_call(
        _dispatch_kernel,
        grid=(T // tm,),
        in_specs=[pl.BlockSpec(memory_space=pl.ANY),
                  pl.BlockSpec((tm, W), lambda i: (i, 0)),
                  pl.BlockSpec(memory_space=pl.ANY)],
        out_specs=pl.BlockSpec(memory_space=pl.ANY),
        out_shape=jax.ShapeDtypeStruct((n_rows_sorted, W), F32),
        scratch_shapes=[pltpu.SMEM((tm,), jnp.int32), pltpu.SemaphoreType.DMA(()), pltpu.SemaphoreType.DMA(())],
        input_output_aliases={2: 0},
        compiler_params=_cparams(("arbitrary",)),
        name="moe_dispatch",
    )(pos, x_aug, jnp.zeros((n_rows_sorted, W), F32))


def _combine_kernel(pos_hbm, ys_hbm, o_ref, pos_smem, idx_sem, sem):
    n_rows = o_ref.shape[0]
    _row_copies(pos_hbm, pos_smem, idx_sem, n_rows,
                lambda r, p: pltpu.make_async_copy(ys_hbm.at[p], o_ref.at[r], sem))


def _combine(ys, pos, tm=2048):
    T = pos.shape[0]
    D = ys.shape[1]
    return pl.pallas_call(
        _combine_kernel,
        grid=(T // tm,),
        in_specs=[pl.BlockSpec(memory_space=pl.ANY), pl.BlockSpec(memory_space=pl.ANY)],
        out_specs=pl.BlockSpec((tm, D), lambda i: (i, 0)),
        out_shape=jax.ShapeDtypeStruct((T, D), F32),
        scratch_shapes=[pltpu.SMEM((tm,), jnp.int32), pltpu.SemaphoreType.DMA(()), pltpu.SemaphoreType.DMA(())],
        compiler_params=_cparams(("arbitrary",)),
        name="moe_combine",
    )(pos, ys)


def _moe_kernel(e_lo_ref, e_hi_ref, used_ref, xs_ref, wg1_ref, wu1_ref, wd1_ref, wg2_ref, wu2_ref, wd2_ref,
                g_ref, beta_ref, ys_ref, *, alpha):
    del e_lo_ref, e_hi_ref
    D = ys_ref.shape[1]
    used = used_ref[pl.program_id(0)] > 0

    @pl.when(used)
    def _():
        x = xs_ref[:, 0:D]
        w = xs_ref[:, D:D + LANES]
        xb = x.astype(BF16)

        def expert(wg_ref, wu_ref, wd_ref):
            act = _silu(_dot(xb, wg_ref[...].astype(BF16))) * _dot(xb, wu_ref[...].astype(BF16))
            return _dot(act.astype(BF16), wd_ref[...].astype(BF16))

        ffn = w[:, 0:1] * expert(wg1_ref, wu1_ref, wd1_ref) + w[:, 1:2] * expert(wg2_ref, wu2_ref, wd2_ref)
        ys_ref[...] = _layer_norm(alpha * x + ffn, g_ref[...], beta_ref[...])

    @pl.when(jnp.logical_not(used))
    def _():
        ys_ref[...] = jnp.zeros_like(ys_ref)


def _moe(xs, e_lo, e_hi, used, w_gate, w_up, w_down, ln_g, ln_b, alpha, tile):
    R, W = xs.shape
    E, D, F = w_gate.shape
    first = lambda i, lo, hi, u: (lo[i], 0, 0)
    second = lambda i, lo, hi, u: (hi[i], 0, 0)
    const = lambda i, lo, hi, u: (0, 0)
    wg, wu, wd = w_gate, w_up, w_down
    grid_spec = pltpu.PrefetchScalarGridSpec(
        num_scalar_prefetch=3,
        grid=(R // tile,),
        in_specs=[pl.BlockSpec((tile, W), lambda i, lo, hi, u: (i, 0)),
                  pl.BlockSpec((None, D, F), first), pl.BlockSpec((None, D, F), first),
                  pl.BlockSpec((None, F, D), first),
                  pl.BlockSpec((None, D, F), second), pl.BlockSpec((None, D, F), second),
                  pl.BlockSpec((None, F, D), second),
                  pl.BlockSpec((1, D), const), pl.BlockSpec((1, D), const)],
        out_specs=pl.BlockSpec((tile, D), lambda i, lo, hi, u: (i, 0)),
    )
    return pl.pallas_call(
        functools.partial(_moe_kernel, alpha=alpha),
        grid_spec=grid_spec,
        out_shape=jax.ShapeDtypeStruct((R, D), F32),
        compiler_params=_cparams(("arbitrary",)),
        name="moe",
    )(e_lo, e_hi, used, xs, wg, wu, wd, wg, wu, wd, ln_g.reshape(1, D), ln_b.reshape(1, D))


def _moe_block(h1_aug, cls, rank, counts, w_gate, w_up, w_down, ln_g, ln_b, alpha, tile=MOE_TILE):
    pos, e_lo, e_hi, used, n_tiles = _plan(cls, rank, counts, tile)
    xs = _dispatch(h1_aug, pos, n_tiles * tile)
    ys = _moe(xs, e_lo, e_hi, used, w_gate, w_up, w_down, ln_g, ln_b, alpha, tile)
    return _combine(ys, pos)


def kernel(x, positions, w_in, gla_gate_up, gla_gate_bias, gla_norm_gain, w_out, ln_mix_g, ln_mix_b, router_w, router_bias, w_expert_gate, w_expert_up, w_expert_down, ln_ffn_g, ln_ffn_b):
    B, L, D = x.shape
    depth = w_in.shape[0]
    alpha = (2 * depth) ** 0.25
    T = B * L
    tabs = _rope_tables(positions)
    h = x.reshape(T, D)
    for layer in range(depth):
        y = _in_proj(h, _pack_w_in(w_in[layer]), tabs)
        out_a = _gla(y, B, L, gla_gate_up[layer], gla_gate_bias[layer], gla_norm_gain[layer])
        y3 = y.reshape(B, L, NP)
        out_b = _moba(y3)
        out_c = _dsa(y3)
        h1_aug, cls, rank, counts = _out_proj(h, out_a, out_b, out_c, w_out[layer], ln_mix_g[layer],
                                              ln_mix_b[layer], router_w, router_bias, alpha)
        h = _moe_block(h1_aug, cls, rank, counts, w_expert_gate[layer], w_expert_up[layer],
                       w_expert_down[layer], ln_ffn_g[layer], ln_ffn_b[layer], alpha)
    return h.reshape(B, L, D)
```
